```python
import math
import jax
import jax.numpy as jnp
from jax import lax
import numpy as np


D_MODEL = 1024
BATCH = 16
SEQ = 2048
DEPTH = 2

CTX_LEN = 256
GRID_W = 64
EPS = 1e-6
N_BRANCH = 3
BRANCH_WIDTH = 512
A_HEADS = 8
A_HEAD_DIM = BRANCH_WIDTH // A_HEADS
DECAY_LORA = 64
AAA_LORA = 64
GATE_LORA = 128
LN_X_EPS = 64e-5
B_HEADS = 4
B_HEAD_DIM = BRANCH_WIDTH // B_HEADS
SHORT_CONV = 5
GDN_CHUNK = 64
DW_CONV = 31
N_EXPERTS = 16
N_GROUPS = 4
EXP_PER_GROUP = N_EXPERTS // N_GROUPS
TOP_K = 2
D_EXPERT = 512
MOE_BLOCK = 128

A_COLS = 3 * BRANCH_WIDTH + 2 * DECAY_LORA + 2 * AAA_LORA + GATE_LORA
B_COLS = 4 * BRANCH_WIDTH + 4 * B_HEADS
C_COLS = 2 * BRANCH_WIDTH
G_COLS = N_BRANCH * D_MODEL
N_IN = A_COLS + B_COLS + C_COLS + G_COLS
A_SPLITS = [BRANCH_WIDTH, 2 * BRANCH_WIDTH, 3 * BRANCH_WIDTH, 3 * BRANCH_WIDTH + 2 * DECAY_LORA, 3 * BRANCH_WIDTH + 2 * DECAY_LORA + 2 * AAA_LORA]
B_SPLITS = [3 * BRANCH_WIDTH, 4 * BRANCH_WIDTH, 4 * BRANCH_WIDTH + 2 * B_HEADS]
IN_SPLITS = [A_COLS, A_COLS + B_COLS, A_COLS + B_COLS + C_COLS]

kernel_name = 'hybrid_rwkv7_gdn_conformer_moe_dit'


def rmsnorm(x, g):
    xf = x.astype(jnp.float32)
    y = xf * lax.rsqrt(jnp.mean(xf * xf, axis=-1, keepdims=True) + EPS)
    return (y * g).astype(x.dtype)


def layernorm(x, g, b, eps=EPS):
    xf = x.astype(jnp.float32)
    mu = jnp.mean(xf, axis=-1, keepdims=True)
    var = jnp.mean(jnp.square(xf - mu), axis=-1, keepdims=True)
    return ((xf - mu) * lax.rsqrt(var + eps) * g + b).astype(x.dtype)


def modulate(x, g, shift, scale):
    return rmsnorm(x, g) * (1 + scale) + shift


def heads(t, n):
    return t.reshape(t.shape[:-1] + (n, t.shape[-1] // n))


def l2norm(t):
    tf = t.astype(jnp.float32)
    return tf * lax.rsqrt(jnp.sum(tf * tf, axis=-1, keepdims=True) + EPS)


def dwconv(x, w):
    K, C = w.shape
    return lax.conv_general_dilated(x, w[:, None, :].astype(x.dtype), window_strides=(1,),
                                    padding=[(K // 2, K // 2)],
                                    dimension_numbers=('NWC', 'WIO', 'NWC'),
                                    feature_group_count=C)


def centred_shift(p, mu):
    prev = jnp.pad(p[:, :-1], ((0, 0), (1, 0), (0, 0)))
    nxt = jnp.pad(p[:, 1:], ((0, 0), (0, 1), (0, 0)))
    return p + mu[0] * (prev - p) + mu[1] * (nxt - p)


def run_bidirectional(scan_fn, ctx_args, lat_args, state0):
    y_ctx, y_lat = [], []
    for d in range(2):
        orient = (lambda t: t) if d == 0 else (lambda t: jnp.flip(t, axis=1))
        yc, s_ctx = scan_fn(state0, *[orient(t) for t in ctx_args[d]])
        yl, _ = scan_fn(s_ctx, *[orient(t) for t in lat_args[d]])
        y_ctx.append(orient(yc))
        y_lat.append(orient(yl))
    return y_ctx[0] + y_ctx[1], y_lat[0] + y_lat[1]


def rwkv_prepare(pa, mu, w0, w2, a0, a2, g2, k_k, k_a):
    pa = centred_shift(pa, mu)
    r, k, v, xw, xa, xg = jnp.split(pa, A_SPLITS, axis=-1)
    bsz, T, _ = r.shape
    xw = xw.reshape(bsz, T, 2, DECAY_LORA)
    xa = xa.reshape(bsz, T, 2, AAA_LORA)
    w_log = -jax.nn.softplus(-(w0 + jnp.einsum('btnr,nrc->btnc', jnp.tanh(xw), w2)).astype(jnp.float32)) - 0.5
    decay = jnp.exp(-jnp.exp(w_log))
    a = jax.nn.sigmoid(a0 + jnp.einsum('btnr,nrc->btnc', xa, a2))
    g = jax.nn.sigmoid(xg) @ g2
    kk = l2norm(heads(k * k_k, A_HEADS))
    k_eff = k[:, :, None] * (1 + (a - 1) * k_a)
    r_h, k_h, v_h = heads(r, A_HEADS), heads(k, A_HEADS), heads(v, A_HEADS)
    dec_h, keff_h, a_h = heads(decay, A_HEADS), heads(k_eff, A_HEADS), heads(a, A_HEADS)
    dir_args = [(r_h, dec_h[:, :, d], keff_h[:, :, d], v_h, kk, a_h[:, :, d]) for d in range(2)]
    return dir_args, (r_h, k_h, v_h, g)


def rwkv7_scan(S0, r, w, k, v, kk, a):
    def step(S, xs):
        r_t, w_t, k_t, v_t, kk_t, a_t = xs
        S = (S * w_t[:, :, None, :]
             - jnp.einsum('bhvk,bhk->bhv', S, kk_t)[..., None] * (kk_t * a_t)[:, :, None, :]
             + v_t[..., None] * k_t[:, :, None, :])
        return S, jnp.einsum('bhvk,bhk->bhv', S, r_t)
    xs = tuple(jnp.moveaxis(t.astype(jnp.float32), 1, 0) for t in (r, w, k, v, kk, a))
    S, y = lax.scan(step, S0, xs)
    return jnp.moveaxis(y, 0, 1), S


def rwkv_output(y, r_h, k_h, v_h, g, r_k, ln_g, ln_b):
    bsz, T = y.shape[:2]
    yn = layernorm(y, ln_g.reshape(A_HEADS, A_HEAD_DIM), ln_b.reshape(A_HEADS, A_HEAD_DIM), LN_X_EPS)
    bonus = jnp.sum(r_h * k_h * r_k, axis=-1, keepdims=True) * v_h
    return (yn + bonus).reshape(bsz, T, BRANCH_WIDTH) * g


def gdn_prepare(pb, conv_w, A_log, dt_bias):
    qkv, z, a_in, b_in = jnp.split(pb, B_SPLITS, axis=-1)
    qkv = jax.nn.silu(dwconv(qkv, conv_w))
    q, k, v = jnp.split(qkv, 3, axis=-1)
    bsz, T, _ = q.shape
    q = l2norm(heads(q, B_HEADS)) * B_HEAD_DIM ** -0.5
    k = l2norm(heads(k, B_HEADS))
    v = heads(v, B_HEADS)
    g = -jnp.exp(A_log.astype(jnp.float32)) * jax.nn.softplus(a_in.reshape(bsz, T, 2, B_HEADS).astype(jnp.float32) + dt_bias)
    beta = jax.nn.sigmoid(b_in.reshape(bsz, T, 2, B_HEADS))
    dir_args = [(q, k, v, g[:, :, d], beta[:, :, d]) for d in range(2)]
    return dir_args, z


def gdn_chunked(S0, q, k, v, g, beta):
    q, k, v, g, beta = (t.astype(jnp.float32) for t in (q, k, v, g, beta))
    bsz, T, H, dk = q.shape
    dv = v.shape[-1]
    n = T // GDN_CHUNK

    def chunked(t):
        t = t.reshape((bsz, n, GDN_CHUNK, H) + t.shape[3:])
        return jnp.moveaxis(jnp.swapaxes(t, 2, 3), 1, 0)

    q, k, v, g, beta = (chunked(t) for t in (q, k, v, g, beta))
    gc = jnp.cumsum(g, axis=-1)
    idx = jnp.arange(GDN_CHUNK)
    lower_incl = idx[:, None] >= idx[None, :]
    lower_strict = idx[:, None] > idx[None, :]
    diff = gc[..., :, None] - gc[..., None, :]
    decay_incl = jnp.where(lower_incl, jnp.exp(jnp.where(lower_incl, diff, 0.0)), 0.0)
    decay_strict = jnp.where(lower_strict, decay_incl, 0.0)
    kb = k * beta[..., None]
    L = jnp.einsum('nbhid,nbhjd->nbhij', kb, k) * decay_strict
    rhs = jnp.concatenate([v * beta[..., None], kb * jnp.exp(gc)[..., None]], axis=-1)
    sol = lax.linalg.triangular_solve(L + jnp.eye(GDN_CHUNK, dtype=jnp.float32), rhs,
                                      left_side=True, lower=True, unit_diagonal=True)
    u, w = sol[..., :dv], sol[..., dv:]
    attn = jnp.einsum('nbhid,nbhjd->nbhij', q, k) * decay_incl
    q_dec = q * jnp.exp(gc)[..., None]
    k_dec = k * jnp.exp(gc[..., -1:] - gc)[..., None]
    g_last = jnp.exp(gc[..., -1])

    def step(S, xs):
        u_c, w_c, attn_c, qd_c, kd_c, gl_c = xs
        v_new = u_c - jnp.einsum('bhik,bhkv->bhiv', w_c, S)
        o = jnp.einsum('bhik,bhkv->bhiv', qd_c, S) + jnp.einsum('bhij,bhjv->bhiv', attn_c, v_new)
        S = S * gl_c[..., None, None] + jnp.einsum('bhik,bhiv->bhkv', kd_c, v_new)
        return S, o

    S, o = lax.scan(step, S0, (u, w, attn, q_dec, k_dec, g_last))
    o = jnp.swapaxes(jnp.moveaxis(o, 0, 1), 2, 3).reshape(bsz, T, H, dv)
    return o, S


def gdn_output(o, z, norm_g):
    bsz, T = o.shape[:2]
    return (rmsnorm(o, norm_g) * jax.nn.silu(heads(z, B_HEADS))).reshape(bsz, T, BRANCH_WIDTH)


def conformer_branch(pc, dw, dw_b, ln_g, ln_b, on_grid):
    val, gate = jnp.split(pc, 2, axis=-1)
    u = val * jax.nn.sigmoid(gate)
    if on_grid:
        bsz, T, C = u.shape
        rows = T // GRID_W
        half = C // 2
        grid = u.reshape(bsz, rows, GRID_W, C)
        along_w = dwconv(grid[..., :half].reshape(bsz * rows, GRID_W, half), dw[:, :half])
        along_h = dwconv(jnp.swapaxes(grid[..., half:], 1, 2).reshape(bsz * GRID_W, rows, C - half), dw[:, half:])
        along_w = along_w.reshape(bsz, rows, GRID_W, half)
        along_h = jnp.swapaxes(along_h.reshape(bsz, GRID_W, rows, C - half), 1, 2)
        y = jnp.concatenate([along_w, along_h], axis=-1).reshape(bsz, T, C)
    else:
        y = dwconv(u, dw)
    return jax.nn.silu(layernorm(y + dw_b, ln_g, ln_b))


def merge_branches(ya, yb, yc, pg, w_branch, w_out):
    y = jnp.stack([ya, yb, yc], axis=2).astype(pg.dtype)
    up = jnp.einsum('btnc,ncd->btnd', y, w_branch)
    gates = jax.nn.sigmoid(pg.reshape(pg.shape[:-1] + (N_BRANCH, D_MODEL)))
    return jnp.sum(gates * up, axis=2) @ w_out


def mixer_sublayer(h, hc, with_ctx_out, w_in, rwkv_p, gdn_p, conf_p, w_branch, w_out):
    mu, w0, w2, a0, a2, g2, k_k, k_a, r_k, lnx_g, lnx_b = rwkv_p
    conv_w, A_log, dt_bias, gdn_g = gdn_p
    bsz = h.shape[0]
    p = h @ w_in
    pa, pb, pc, pg = jnp.split(p, IN_SPLITS, axis=-1)
    ctx_cols = N_IN if with_ctx_out else A_COLS + B_COLS
    p_ctx = hc @ w_in[:, :ctx_cols]
    pa_c, pb_c = p_ctx[..., :A_COLS], p_ctx[..., A_COLS:A_COLS + B_COLS]
    rw = (mu, w0, w2, a0, a2, g2, k_k, k_a)
    a_args, a_aux = rwkv_prepare(pa, *rw)
    a_args_c, a_aux_c = rwkv_prepare(pa_c, *rw)
    ya_c, ya = run_bidirectional(rwkv7_scan, a_args_c, a_args,
                                 jnp.zeros((bsz, A_HEADS, A_HEAD_DIM, A_HEAD_DIM), jnp.float32))
    b_args, z = gdn_prepare(pb, conv_w, A_log, dt_bias)
    b_args_c, z_c = gdn_prepare(pb_c, conv_w, A_log, dt_bias)
    yb_c, yb = run_bidirectional(gdn_chunked, b_args_c, b_args,
                                 jnp.zeros((bsz, B_HEADS, B_HEAD_DIM, B_HEAD_DIM), jnp.float32))
    mix = merge_branches(rwkv_output(ya, *a_aux, r_k, lnx_g, lnx_b), gdn_output(yb, z, gdn_g),
                         conformer_branch(pc, *conf_p, True), pg, w_branch, w_out)
    if not with_ctx_out:
        return mix, None
    pc_c = p_ctx[..., A_COLS + B_COLS:A_COLS + B_COLS + C_COLS]
    pg_c = p_ctx[..., A_COLS + B_COLS + C_COLS:]
    mix_c = merge_branches(rwkv_output(ya_c, *a_aux_c, r_k, lnx_g, lnx_b), gdn_output(yb_c, z_c, gdn_g),
                           conformer_branch(pc_c, *conf_p, False), pg_c, w_branch, w_out)
    return mix, mix_c


def moe_ffn(h, w_router, router_bias, w_gate, w_up, w_down):
    bsz, T, D = h.shape
    xf = h.reshape(bsz * T, D)
    N = xf.shape[0]
    scores = jax.nn.sigmoid(xf.astype(jnp.float32) @ w_router.astype(jnp.float32))
    sel = (scores + router_bias.astype(jnp.float32)).reshape(N, N_GROUPS, EXP_PER_GROUP)
    group_score = jnp.sum(lax.top_k(sel, TOP_K)[0], axis=-1)
    grp = jnp.argmax(group_score, axis=-1)
    sel_in_grp = jnp.take_along_axis(sel, grp[:, None, None], axis=1)[:, 0]
    _, local = lax.top_k(sel_in_grp, TOP_K)
    experts = grp[:, None] * EXP_PER_GROUP + local
    wts = jnp.take_along_axis(scores, experts, axis=1)
    wts = wts / jnp.sum(wts, axis=-1, keepdims=True)
    A = N * TOP_K
    flat_e = experts.reshape(A)
    flat_tok = jnp.repeat(jnp.arange(N, dtype=jnp.int32), TOP_K)
    flat_w = wts.reshape(A)
    order = jnp.argsort(flat_e)
    e_sorted = flat_e[order]
    sizes = jnp.bincount(flat_e, length=N_EXPERTS)
    starts = jnp.cumsum(sizes) - sizes
    padded = (sizes + MOE_BLOCK - 1) // MOE_BLOCK * MOE_BLOCK
    pad_ends = jnp.cumsum(padded)
    pad_starts = pad_ends - padded
    dest = pad_starts[e_sorted] + jnp.arange(A, dtype=jnp.int32) - starts[e_sorted]
    n_blocks = -(-A // MOE_BLOCK) + N_EXPERTS
    rows = n_blocks * MOE_BLOCK
    row_tok = jnp.full((rows,), N, jnp.int32).at[dest].set(flat_tok[order])
    row_w = jnp.zeros((rows,), jnp.float32).at[dest].set(flat_w[order])
    block_e = jnp.minimum(jnp.searchsorted(pad_ends, jnp.arange(n_blocks) * MOE_BLOCK, side='right'), N_EXPERTS - 1)
    x_pad = jnp.concatenate([xf, jnp.zeros((1, D), xf.dtype)], axis=0)
    xb = x_pad[row_tok].reshape(n_blocks, MOE_BLOCK, D)

    def expert_block(args):
        xblk, e = args
        return (jax.nn.silu(xblk @ w_gate[e]) * (xblk @ w_up[e])) @ w_down[e]

    yb = lax.map(expert_block, (xb, block_e)).reshape(rows, D)
    y = jnp.zeros((N + 1, D), yb.dtype).at[row_tok].add(yb * row_w[:, None].astype(yb.dtype))
    return y[:N].reshape(bsz, T, D)


def setup_inputs(seed: int = 0) -> dict:
    key = jax.random.key(seed)
    keys = jax.random.split(key, 36)
    L, D = DEPTH, D_MODEL

    def nrm(i, shape, scale):
        return jax.random.normal(keys[i], shape, jnp.float32) * scale

    def uni(i, shape, lo, hi):
        return jax.random.uniform(keys[i], shape, jnp.float32, lo, hi)

    dt = jnp.exp(uni(23, (L, 2, B_HEADS), math.log(1e-3), math.log(1e-1)))
    return {
        'x': nrm(0, (BATCH, SEQ, D), 1.0),
        'c': nrm(1, (BATCH, D), 1.0),
        'ctx': nrm(2, (BATCH, CTX_LEN, D), 1.0),
        'c_ctx': nrm(3, (D,), 1.0),
        'w_ada': nrm(4, (L, D, 6 * D), 0.5 * D ** -0.5),
        'b_ada': nrm(5, (L, 6 * D), 0.02),
        'norm_mix': 1.0 + nrm(6, (L, D), 0.02),
        'norm_ffn': 1.0 + nrm(7, (L, D), 0.02),
        'norm_final': 1.0 + nrm(8, (D,), 0.02),
        'w_in': nrm(9, (L, D, N_IN), D ** -0.5),
        'rwkv_mu': uni(10, (L, 2, A_COLS), 0.0, 0.5),
        'rwkv_w0': uni(11, (L, 2, BRANCH_WIDTH), -6.0, 0.0),
        'rwkv_w2': nrm(12, (L, 2, DECAY_LORA, BRANCH_WIDTH), 0.1 * DECAY_LORA ** -0.5),
        'rwkv_a0': uni(13, (L, 2, BRANCH_WIDTH), -1.0, 1.0),
        'rwkv_a2': nrm(14, (L, 2, AAA_LORA, BRANCH_WIDTH), 0.5 * AAA_LORA ** -0.5),
        'rwkv_g2': nrm(15, (L, GATE_LORA, BRANCH_WIDTH), GATE_LORA ** -0.5),
        'rwkv_kk': 0.85 + nrm(16, (L, BRANCH_WIDTH), 0.02),
        'rwkv_ka': 1.0 + nrm(17, (L, BRANCH_WIDTH), 0.02),
        'rwkv_rk': nrm(18, (L, A_HEADS, A_HEAD_DIM), 0.1),
        'rwkv_ln_g': 1.0 + nrm(19, (L, BRANCH_WIDTH), 0.02),
        'rwkv_ln_b': nrm(20, (L, BRANCH_WIDTH), 0.02),
        'gdn_conv': nrm(21, (L, SHORT_CONV, 3 * BRANCH_WIDTH), SHORT_CONV ** -0.5),
        'gdn_A_log': jnp.log(uni(22, (L, 2, B_HEADS), 1.0, 16.0)),
        'gdn_dt_bias': dt + jnp.log(-jnp.expm1(-dt)),
        'gdn_norm': 1.0 + nrm(24, (L, B_HEAD_DIM), 0.02),
        'conf_dw': nrm(25, (L, DW_CONV, BRANCH_WIDTH), DW_CONV ** -0.5),
        'conf_dw_b': nrm(26, (L, BRANCH_WIDTH), 0.02),
        'conf_ln_g': 1.0 + nrm(27, (L, BRANCH_WIDTH), 0.02),
        'conf_ln_b': nrm(28, (L, BRANCH_WIDTH), 0.02),
        'w_branch': nrm(29, (L, N_BRANCH, BRANCH_WIDTH, D), BRANCH_WIDTH ** -0.5),
        'w_out': nrm(30, (L, D, D), D ** -0.5),
        'w_router': nrm(31, (D, N_EXPERTS), D ** -0.5),
        'router_bias': nrm(32, (N_EXPERTS,), 0.01),
        'w_e_gate': nrm(33, (L, N_EXPERTS, D, D_EXPERT), D ** -0.5),
        'w_e_up': nrm(34, (L, N_EXPERTS, D, D_EXPERT), D ** -0.5),
        'w_e_down': nrm(35, (L, N_EXPERTS, D_EXPERT, D), D_EXPERT ** -0.5),
    }


def reference(x, c, ctx, c_ctx, w_ada, b_ada, norm_mix, norm_ffn, norm_final, w_in,
              rwkv_mu, rwkv_w0, rwkv_w2, rwkv_a0, rwkv_a2, rwkv_g2, rwkv_kk, rwkv_ka, rwkv_rk,
              rwkv_ln_g, rwkv_ln_b, gdn_conv, gdn_A_log, gdn_dt_bias, gdn_norm,
              conf_dw, conf_dw_b, conf_ln_g, conf_ln_b, w_branch, w_out,
              w_router, router_bias, w_e_gate, w_e_up, w_e_down):
    silu_c = jax.nn.silu(c)
    silu_cc = jax.nn.silu(c_ctx)
    xc = ctx
    for l in range(DEPTH):
        with_ctx_out = l < DEPTH - 1
        mod = jnp.split((silu_c @ w_ada[l] + b_ada[l])[:, None, :], 6, axis=-1)
        mod_c = jnp.split(silu_cc @ w_ada[l] + b_ada[l], 6, axis=-1)
        h = modulate(x, norm_mix[l], mod[0], mod[1])
        hc = modulate(xc, norm_mix[l], mod_c[0], mod_c[1])
        rwkv_p = (rwkv_mu[l], rwkv_w0[l], rwkv_w2[l], rwkv_a0[l], rwkv_a2[l], rwkv_g2[l],
                  rwkv_kk[l], rwkv_ka[l], rwkv_rk[l], rwkv_ln_g[l], rwkv_ln_b[l])
        gdn_p = (gdn_conv[l], gdn_A_log[l], gdn_dt_bias[l], gdn_norm[l])
        conf_p = (conf_dw[l], conf_dw_b[l], conf_ln_g[l], conf_ln_b[l])
        mix, mix_c = mixer_sublayer(h, hc, with_ctx_out, w_in[l], rwkv_p, gdn_p, conf_p, w_branch[l], w_out[l])
        x = x + mod[2] * mix
        x = x + mod[5] * moe_ffn(modulate(x, norm_ffn[l], mod[3], mod[4]),
                                 w_router, router_bias, w_e_gate[l], w_e_up[l], w_e_down[l])
        if with_ctx_out:
            xc = xc + mod_c[2] * mix_c
            xc = xc + mod_c[5] * moe_ffn(modulate(xc, norm_ffn[l], mod_c[3], mod_c[4]),
                                         w_router, router_bias, w_e_gate[l], w_e_up[l], w_e_down[l])
    return rmsnorm(x, norm_final)
```

```python
import functools
import math

import jax
import jax.numpy as jnp
from jax import lax
from jax.experimental import pallas as pl
from jax.experimental.pallas import tpu as pltpu

F32 = jnp.float32
BF16 = jnp.bfloat16
I32 = jnp.int32
HIGHEST = lax.Precision.HIGHEST

EPS = 1e-6
LN_X_EPS = 64e-5
GRID_W = 64
BRANCH_WIDTH = 512
A_HEADS = 8
A_HEAD_DIM = 64
B_HEADS = 4
B_HEAD_DIM = 128
DECAY_LORA = 64
SHORT_CONV = 5
DW_CONV = 31
N_EXPERTS = 16
N_GROUPS = 4
EXP_PER_GROUP = 4
D_EXPERT = 512

ROW_TILE = 256
CHUNK = 64
MOE_ROWS = 128
LANES = 128
SUBLANES = 8
VMEM_LIMIT = 56 * 1024 * 1024

A_COLS = 1920
COL_A = 0
COL_AB = 1920
COL_Q = 2048
COL_Z = 3584
COL_C = 4096
COL_G = 5120
N_IN_PAD = 8192
LORA_W = 3 * LANES


def _cparams(sem):
    return pltpu.CompilerParams(dimension_semantics=sem, vmem_limit_bytes=VMEM_LIMIT)


def _sigmoid(x):
    return 1.0 / (1.0 + jnp.exp(-x))


def _silu(x):
    return x * _sigmoid(x)


def _softplus(x):
    return jnp.maximum(x, 0.0) + jnp.log(1.0 + jnp.exp(-jnp.abs(x)))


def _dot(a, b):
    return jnp.dot(a, b, preferred_element_type=F32)


def _dot_nt(a, b):
    return lax.dot_general(a, b, (((1,), (1,)), ((), ())), preferred_element_type=F32)


def _bf16_parts(x, n):
    parts = []
    r = x
    for _ in range(n):
        p = r.astype(BF16)
        parts.append(p)
        r = r - p.astype(F32)
    return parts


def _dot_sel_left(m, x, n=3):
    acc = None
    for p in _bf16_parts(x, n):
        t = _dot(m, p)
        acc = t if acc is None else acc + t
    return acc


def _dot_sel_right(x, m, n=2):
    acc = None
    for p in _bf16_parts(x, n):
        t = _dot(p, m)
        acc = t if acc is None else acc + t
    return acc


def _group_ones(n, group):
    r = lax.broadcasted_iota(I32, (n, n), 0) // group
    c = lax.broadcasted_iota(I32, (n, n), 1) // group
    return jnp.where(r == c, 1.0, 0.0).astype(BF16)


def _chunk_tri(n, lower):
    r = lax.broadcasted_iota(I32, (n, n), 0)
    c = lax.broadcasted_iota(I32, (n, n), 1)
    same = (r // CHUNK) == (c // CHUNK)
    keep = (c <= r) if lower else (c >= r)
    return jnp.where(same & keep, 1.0, 0.0).astype(BF16)


def _shift_down(x, prev_rows, s):
    n = x.shape[0]
    out = pltpu.roll(x, s, axis=0)
    row = lax.broadcasted_iota(I32, (n, 1), 0)
    for q in range(s):
        out = jnp.where(row == q, prev_rows[SUBLANES - s + q:SUBLANES - s + q + 1, :], out)
    return out


def _shift_up(x, next_rows, s):
    n = x.shape[0]
    out = pltpu.roll(x, n - s, axis=0)
    row = lax.broadcasted_iota(I32, (n, 1), 0)
    for q in range(s):
        out = jnp.where(row == n - s + q, next_rows[q:q + 1, :], out)
    return out


def _halo_rows(p_prev_ref, p_next_ref, i, n_tiles):
    prev_rows = p_prev_ref[0]
    next_rows = p_next_ref[0]
    prev_rows = jnp.where(i <= 1, 0.0, prev_rows)
    next_rows = jnp.where((i == 0) | (i == n_tiles - 1), 0.0, next_rows)
    return prev_rows, next_rows


def _halo_specs(width, col, tiles_per_batch):
    assert col % width == 0
    col_block = col // width
    sub_per_tile = ROW_TILE // SUBLANES
    last = tiles_per_batch * sub_per_tile - 1
    main = pl.BlockSpec((1, ROW_TILE, width), lambda b, i: (b, i, col_block))
    prev = pl.BlockSpec((1, SUBLANES, width),
                        lambda b, i: (b, jnp.maximum(i * sub_per_tile - 1, 0), col_block))
    nxt = pl.BlockSpec((1, SUBLANES, width),
                       lambda b, i: (b, jnp.minimum((i + 1) * sub_per_tile, last), col_block))
    return [main, prev, nxt]


def _ada_kernel(s_ref, w_ref, b_ref, o_ref):
    s = s_ref[...]
    o_ref[0] = jnp.dot(_silu(s), w_ref[0], preferred_element_type=F32, precision=HIGHEST) + b_ref[0]


def _ada(svec, w_ada, b_ada):
    L, D, N = w_ada.shape
    rows = svec.shape[0]
    tn = 1536
    return pl.pallas_call(
        _ada_kernel,
        grid=(L, N // tn),
        in_specs=[pl.BlockSpec((rows, D), lambda l, j: (0, 0)),
                  pl.BlockSpec((1, D, tn), lambda l, j: (l, 0, j)),
                  pl.BlockSpec((1, 1, tn), lambda l, j: (l, 0, j))],
        out_specs=pl.BlockSpec((1, rows, tn), lambda l, j: (l, 0, j)),
        out_shape=jax.ShapeDtypeStruct((L, rows, N), F32),
        compiler_params=_cparams(("parallel", "parallel")),
        name="ada",
    )(svec, w_ada, b_ada.reshape(L, 1, N))


def _modulated(x, gain, mods, row0, ctx_len):
    sh_b, sc_b, sh_c, sc_c = mods
    y = x * lax.rsqrt(jnp.mean(x * x, axis=-1, keepdims=True) + EPS) * gain
    row = row0 + lax.broadcasted_iota(I32, (x.shape[0], 1), 0)
    isc = row < ctx_len
    shift = jnp.where(isc, sh_c, sh_b)
    scale = jnp.where(isc, sc_c, sc_b)
    return y * (1.0 + scale) + shift


def _mod_specs(n_grid, ctx_row, chunks, d):
    def spec(row_of, c):
        if n_grid == 2:
            return pl.BlockSpec((1, 1, d), lambda b, i: (row_of(b), 0, c))
        return pl.BlockSpec((1, 1, d), lambda b, i, j: (row_of(b), 0, c))
    return ([spec(lambda b: b, c) for c in chunks] + [spec(lambda b: ctx_row, c) for c in chunks])


def _inproj_kernel(x_ref, g_ref, shb_ref, scb_ref, shc_ref, scc_ref, w_ref, o_ref, h_scr, *, tm, ctx_len):
    i = pl.program_id(1)
    j = pl.program_id(2)

    @pl.when(j == 0)
    def _():
        mods = (shb_ref[0], scb_ref[0], shc_ref[0], scc_ref[0])
        h_scr[...] = _modulated(x_ref[0], g_ref[...], mods, i * tm, ctx_len).astype(BF16)

    o_ref[0] = _dot(h_scr[...], w_ref[...])


def _inproj(x, gain, mods, ctx_row, wp, ctx_len):
    B, S, D = x.shape
    N = wp.shape[1]
    tm = S // 2
    tn = 1024
    return pl.pallas_call(
        functools.partial(_inproj_kernel, tm=tm, ctx_len=ctx_len),
        grid=(B, S // tm, N // tn),
        in_specs=[pl.BlockSpec((1, tm, D), lambda b, i, j: (b, i, 0)),
                  pl.BlockSpec((1, D), lambda b, i, j: (0, 0))]
                 + _mod_specs(3, ctx_row, (0, 1), D)
                 + [pl.BlockSpec((D, tn), lambda b, i, j: (0, j))],
        out_specs=pl.BlockSpec((1, tm, tn), lambda b, i, j: (b, i, j)),
        out_shape=jax.ShapeDtypeStruct((B, S, N), F32),
        scratch_shapes=[pltpu.VMEM((tm, D), BF16)],
        compiler_params=_cparams(("parallel", "parallel", "arbitrary")),
        name="inproj",
    )(x, gain.reshape(1, D), mods, mods, mods, mods, wp)


def _rwkv_prep_kernel(r_ref, rp_ref, rn_ref, k_ref, kp_ref_, kn_ref, vv_ref, vp_ref, vn_ref,
                      x_ref, xp_ref, xn_ref,
                      mu_ref, w0_ref, w2_ref, a0_ref, a2_ref, g2_ref, kk_ref, ka_ref, rk_ref,
                      rt0_ref, kp0_ref, bt0_ref, kt0_ref, pc0_ref,
                      rt1_ref, kp1_ref, bt1_ref, kt1_ref, pc1_ref,
                      v_ref, bonus_ref, g_ref, *, n_tiles):
    i = pl.program_id(1)
    W = BRANCH_WIDTH

    def shifted(main, prv, nxt, col, width):
        p = main[0]
        prev_rows, next_rows = _halo_rows(prv, nxt, i, n_tiles)
        mu = mu_ref[:, col:col + width]
        return (p + mu[0:1] * (_shift_down(p, prev_rows, 1) - p)
                + mu[1:2] * (_shift_up(p, next_rows, 1) - p))

    r = shifted(r_ref, rp_ref, rn_ref, 0, W)
    k = shifted(k_ref, kp_ref_, kn_ref, W, W)
    v = shifted(vv_ref, vp_ref, vn_ref, 2 * W, W)
    lora = shifted(x_ref, xp_ref, xn_ref, 3 * W, LORA_W)
    xw = jnp.tanh(lora[:, 0:LANES])
    xa = lora[:, LANES:2 * LANES]
    xg = _sigmoid(lora[:, 2 * LANES:3 * LANES])
    lane_hi = lax.broadcasted_iota(I32, xw.shape, 1) >= DECAY_LORA

    g_ref[0] = _dot(xg.astype(BF16), g2_ref[...].astype(BF16))
    v_ref[0] = v.astype(BF16)

    gsum = _group_ones(W, A_HEAD_DIM)
    kx = k * kk_ref[...]
    kk = kx * lax.rsqrt(_dot_sel_right(kx * kx, gsum) + EPS)
    bonus_ref[0] = _dot_sel_right(r * k * rk_ref[...], gsum) * v

    w2 = w2_ref[...].astype(BF16)
    a2 = a2_ref[...].astype(BF16)
    n = r.shape[0]
    outs = ((rt0_ref, kp0_ref, bt0_ref, kt0_ref, pc0_ref), (rt1_ref, kp1_ref, bt1_ref, kt1_ref, pc1_ref))
    for d in range(2):
        keep = lane_hi if d == 1 else jnp.logical_not(lane_hi)
        lw = w0_ref[d:d + 1, :] + _dot(jnp.where(keep, xw, 0.0).astype(BF16), w2)
        ew = _sigmoid(lw) * math.exp(-0.5)
        a = _sigmoid(a0_ref[d:d + 1, :] + _dot(jnp.where(keep, xa, 0.0).astype(BF16), a2))
        keff = k * (1.0 + (a - 1.0) * ka_ref[...])
        cs = _dot_sel_left(_chunk_tri(n, lower=(d == 0)), ew)
        p_incl = jnp.exp(-cs)
        p_excl = jnp.exp(ew - cs)
        p_inv = jnp.exp(cs)
        rt_ref, kp_ref, bt_ref, kt_ref, pc_ref = outs[d]
        rt_ref[0] = (r * p_incl).astype(BF16)
        kp_ref[0] = (kk * p_excl).astype(BF16)
        bt_ref[0] = (kk * a * p_inv).astype(BF16)
        kt_ref[0] = (keff * p_inv).astype(BF16)
        for c in range(n // CHUNK):
            row = c * CHUNK + (CHUNK - 1 if d == 0 else 0)
            pc_ref[0, c] = p_incl[row:row + 1, :]


def _rwkv_prep(P, mu, w0, w2, a0, a2, g2, k_k, k_a, r_k):
    B, S, _ = P.shape
    W = BRANCH_WIDTH
    nt = S // ROW_TILE
    cpt = ROW_TILE // CHUNK
    halo = (_halo_specs(W, COL_A, nt) + _halo_specs(W, COL_A + W, nt) + _halo_specs(W, COL_A + 2 * W, nt)
            + _halo_specs(LORA_W, COL_A + 3 * W, nt))
    full = lambda shape: pl.BlockSpec(shape, lambda b, i: tuple(0 for _ in shape))
    tok = pl.BlockSpec((1, ROW_TILE, W), lambda b, i: (b, i, 0))
    pcs = pl.BlockSpec((1, cpt, 1, W), lambda b, i: (b, i, 0, 0))
    bf = jax.ShapeDtypeStruct((B, S, W), BF16)
    f32 = jax.ShapeDtypeStruct((B, S, W), F32)
    pcshape = jax.ShapeDtypeStruct((B, S // CHUNK, 1, W), F32)
    return pl.pallas_call(
        functools.partial(_rwkv_prep_kernel, n_tiles=nt),
        grid=(B, nt),
        in_specs=halo + [full((2, A_COLS)), full((2, W)), full((2 * DECAY_LORA, W)), full((2, W)),
                         full((2 * DECAY_LORA, W)), full((LANES, W)), full((1, W)), full((1, W)),
                         full((1, W))],
        out_specs=[tok, tok, tok, tok, pcs, tok, tok, tok, tok, pcs, tok, tok, tok],
        out_shape=[bf, bf, bf, bf, pcshape, bf, bf, bf, bf, pcshape, bf, f32, f32],
        compiler_params=_cparams(("parallel", "parallel")),
        name="rwkv_prep",
    )(*([P] * 12), mu, w0, w2.reshape(2 * DECAY_LORA, W), a0, a2.reshape(2 * DECAY_LORA, W), g2,
      k_k.reshape(1, W), k_a.reshape(1, W), r_k.reshape(1, W))


def _unit_lower_inverse(a_neg, steps):
    n = a_neg.shape[0]
    eye = jnp.where(lax.broadcasted_iota(I32, (n, n), 0) == lax.broadcasted_iota(I32, (n, n), 1), 1.0, 0.0)
    t = eye + a_neg
    xp = a_neg.astype(BF16)
    for _ in range(steps - 1):
        x2 = _dot(xp, xp)
        xp = x2.astype(BF16)
        t = t + _dot(t.astype(BF16), xp)
    return t


def _stack_heads(x, lane_hi):
    z = jnp.zeros_like(x)
    return jnp.concatenate([jnp.where(lane_hi, z, x), jnp.where(lane_hi, x, z)], axis=0)


def _rwkv_scan_kernel(rt0, kp0, bt0, kt0, v0, pc0, rt1, kp1, bt1, kt1, v1, pc1, y0_ref, y1_ref, st_ref):
    s = pl.program_id(1)

    @pl.when(s == 0)
    def _():
        st_ref[...] = jnp.zeros_like(st_ref)

    C = CHUNK
    n2 = 2 * C
    ri = lax.broadcasted_iota(I32, (n2, n2), 0)
    ci = lax.broadcasted_iota(I32, (n2, n2), 1)
    same = (ri // C) == (ci // C)
    tr = ri % C
    ts = ci % C
    lane_hi = lax.broadcasted_iota(I32, (C, n2), 1) >= C
    dirs = ((rt0, kp0, bt0, kt0, v0, pc0, y0_ref), (rt1, kp1, bt1, kt1, v1, pc1, y1_ref))
    for d, (rt, kp, bt, kt, vv, pc, y_ref) in enumerate(dirs):
        strict = same & ((ts < tr) if d == 0 else (ts > tr))
        incl = same & ((ts <= tr) if d == 0 else (ts >= tr))
        for j in range(BRANCH_WIDTH // n2):
            sl = slice(n2 * j, n2 * (j + 1))
            kap = kp[0, :, sl]
            r_ = rt[0, :, sl]
            b_ = bt[0, :, sl]
            k_ = kt[0, :, sl]
            v_ = vv[0, :, sl]
            kap_st = _stack_heads(kap, lane_hi)
            r_st = _stack_heads(r_, lane_hi)
            b_st = _stack_heads(b_, lane_hi)
            k_st = _stack_heads(k_, lane_hi)
            v_st = _stack_heads(v_, lane_hi)
            aa = _dot_nt(jnp.concatenate([kap_st, r_st], axis=0), jnp.concatenate([b_, b_, k_, k_], axis=0))
            a_ab = jnp.where(strict, aa[:n2, :n2], 0.0)
            a_ak = jnp.where(strict, aa[:n2, n2:], 0.0)
            a_rb = jnp.where(incl, aa[n2:, :n2], 0.0).astype(BF16)
            a_rk = jnp.where(incl, aa[n2:, n2:], 0.0).astype(BF16)
            t_inv = _unit_lower_inverse(-a_ab, 6)
            t_b = t_inv.astype(BF16)
            uv = -_dot(t_b, _dot(a_ak.astype(BF16), v_st).astype(BF16))
            y_base = _dot(a_rb, uv.astype(BF16)) + _dot(a_rk, v_st)
            g_b = _dot(a_rb, t_b).astype(BF16)
            b_t = b_st.astype(F32).T.astype(BF16)
            k_t = k_st.astype(F32).T.astype(BF16)
            kv0 = _dot(k_t, v_st)
            pcm = jnp.broadcast_to(pc[0, 0][:, sl], (n2, n2)).T
            h = st_ref[d, j]
            krh = _dot(jnp.concatenate([kap, r_], axis=0), h.astype(BF16))
            kh_st = _stack_heads(krh[:C], lane_hi).astype(BF16)
            tg = _dot(jnp.concatenate([t_b, g_b], axis=0), kh_st)
            u_st = uv - tg[:n2]
            y_st = y_base - tg[n2:]
            y_ref[0, :, sl] = y_st[:C] + y_st[C:] + krh[C:]
            st_ref[d, j] = pcm * (h + kv0 + _dot(b_t, u_st.astype(BF16)))


def _scan_chunk_maps(nc_ctx, nc):
    def c0(s):
        return s

    def c1(s):
        return jnp.where(s < nc_ctx, nc_ctx - 1 - s, nc + nc_ctx - 1 - s)

    return c0, c1


def _rwkv_scan(prep, ctx_len):
    rt0, kp0, bt0, kt0, pc0, rt1, kp1, bt1, kt1, pc1, v = prep
    B, S, W = v.shape
    nc = S // CHUNK
    c0, c1 = _scan_chunk_maps(ctx_len // CHUNK, nc)
    tok = lambda cm: pl.BlockSpec((1, CHUNK, W), lambda b, s: (b, cm(s), 0))
    pcs = lambda cm: pl.BlockSpec((1, 1, 1, W), lambda b, s: (b, cm(s), 0, 0))
    y = jax.ShapeDtypeStruct((B, S, W), F32)
    return pl.pallas_call(
        _rwkv_scan_kernel,
        grid=(B, nc),
        in_specs=[tok(c0), tok(c0), tok(c0), tok(c0), tok(c0), pcs(c0),
                  tok(c1), tok(c1), tok(c1), tok(c1), tok(c1), pcs(c1)],
        out_specs=[tok(c0), tok(c1)],
        out_shape=[y, y],
        scratch_shapes=[pltpu.VMEM((2, W // (2 * CHUNK), 2 * CHUNK, 2 * CHUNK), F32)],
        compiler_params=_cparams(("parallel", "arbitrary")),
        name="rwkv_scan",
    )(rt0, kp0, bt0, kt0, v, pc0, rt1, kp1, bt1, kt1, v, pc1)


def _gdn_prep_kernel(qi_ref, qp_ref, qn_ref, ki_ref, kp_ref, kn_ref, vi_ref, vp_ref, vn_ref,
                     ab_ref, cw_ref, al_ref, dt_ref,
                     q_ref, k_ref, v_ref, gcol_ref, grow_ref, *, n_tiles):
    i = pl.program_id(1)
    W = BRANCH_WIDTH
    half = SHORT_CONV // 2

    def conv_silu(main, prv, nxt, col):
        x = main[0]
        prev_rows, next_rows = _halo_rows(prv, nxt, i, n_tiles)
        cw = cw_ref[:, col:col + W]
        acc = cw[half:half + 1] * x
        for s in range(1, half + 1):
            acc = acc + cw[half - s:half - s + 1] * _shift_down(x, prev_rows, s)
            acc = acc + cw[half + s:half + s + 1] * _shift_up(x, next_rows, s)
        return _silu(acc)

    yq = conv_silu(qi_ref, qp_ref, qn_ref, 0)
    yk = conv_silu(ki_ref, kp_ref, kn_ref, W)
    v_ref[0] = conv_silu(vi_ref, vp_ref, vn_ref, 2 * W)
    for hd in range(B_HEADS):
        sl = slice(hd * B_HEAD_DIM, (hd + 1) * B_HEAD_DIM)
        q = yq[:, sl]
        q_ref[0, :, sl] = q * lax.rsqrt(jnp.sum(q * q, axis=-1, keepdims=True) + EPS) * (B_HEAD_DIM ** -0.5)
        k = yk[:, sl]
        k_ref[0, :, sl] = k * lax.rsqrt(jnp.sum(k * k, axis=-1, keepdims=True) + EPS)

    ab = ab_ref[0]
    n = ab.shape[0]
    lane = lax.broadcasted_iota(I32, ab.shape, 1)
    g = -jnp.exp(al_ref[...]) * _softplus(ab + dt_ref[...])
    g = jnp.where(lane < 2 * B_HEADS, g, 0.0)
    gc_f = _dot_sel_left(_chunk_tri(n, lower=True), g)
    gc_b = _dot_sel_left(_chunk_tri(n, lower=False), g)
    gc = jnp.where(lane < B_HEADS, gc_f, gc_b)
    beta = _sigmoid(ab)
    out = jnp.where(lane < 2 * B_HEADS, gc, jnp.where(lane < 4 * B_HEADS, beta, 0.0))
    gcol_ref[0] = out
    grow_ref[0] = out.T[:4 * B_HEADS, :]


def _gdn_prep(P, conv_w, A_log, dt_bias):
    B, S, _ = P.shape
    W = BRANCH_WIDTH
    nt = S // ROW_TILE
    halo = _halo_specs(W, COL_Q, nt) + _halo_specs(W, COL_Q + W, nt) + _halo_specs(W, COL_Q + 2 * W, nt)
    pad = jnp.zeros((1, LANES - 2 * B_HEADS), F32)
    al = jnp.concatenate([A_log.reshape(1, 2 * B_HEADS), pad], axis=1)
    dt = jnp.concatenate([dt_bias.reshape(1, 2 * B_HEADS), pad], axis=1)
    full = lambda shape: pl.BlockSpec(shape, lambda b, i: tuple(0 for _ in shape))
    tok = pl.BlockSpec((1, ROW_TILE, W), lambda b, i: (b, i, 0))
    f32 = jax.ShapeDtypeStruct((B, S, W), F32)
    return pl.pallas_call(
        functools.partial(_gdn_prep_kernel, n_tiles=nt),
        grid=(B, nt),
        in_specs=halo + [pl.BlockSpec((1, ROW_TILE, LANES), lambda b, i: (b, i, COL_AB // LANES)),
                         full((SHORT_CONV, 3 * W)), full((1, LANES)), full((1, LANES))],
        out_specs=[tok, tok, tok,
                   pl.BlockSpec((1, ROW_TILE, LANES), lambda b, i: (b, i, 0)),
                   pl.BlockSpec((1, 4 * B_HEADS, ROW_TILE), lambda b, i: (b, 0, i))],
        out_shape=[f32, f32, f32, jax.ShapeDtypeStruct((B, S, LANES), F32),
                   jax.ShapeDtypeStruct((B, 4 * B_HEADS, S), F32)],
        compiler_params=_cparams(("parallel", "parallel")),
        name="gdn_prep",
    )(*([P] * 10), conv_w, al, dt)


def _gdn_scan_kernel(q0, k0, v0, gc0, gr0, q1, k1, v1, gc1, gr1, o0_ref, o1_ref, st_ref):
    s = pl.program_id(1)

    @pl.when(s == 0)
    def _():
        st_ref[...] = jnp.zeros_like(st_ref)

    C = CHUNK
    ri = lax.broadcasted_iota(I32, (C, C), 0)
    ci = lax.broadcasted_iota(I32, (C, C), 1)
    dirs = ((q0, k0, v0, gc0, gr0, o0_ref), (q1, k1, v1, gc1, gr1, o1_ref))
    for d, (qr, kr, vr, gcr, grr, o_ref) in enumerate(dirs):
        incl = (ci <= ri) if d == 0 else (ci >= ri)
        strict = (ci < ri) if d == 0 else (ci > ri)
        last = C - 1 if d == 0 else 0
        gcol = gcr[0]
        grow = grr[0, 0]
        for hd in range(B_HEADS):
            sl = slice(hd * B_HEAD_DIM, (hd + 1) * B_HEAD_DIM)
            idx = d * B_HEADS + hd
            q = qr[0, :, sl]
            k = kr[0, :, sl]
            v = vr[0, :, sl]
            gc_c = gcol[:, idx:idx + 1]
            gc_r = grow[idx:idx + 1, :]
            beta = gcol[:, 2 * B_HEADS + idx:2 * B_HEADS + idx + 1]
            gc_last = gc_r[:, last:last + 1]
            diff = gc_c - gc_r
            decay_incl = jnp.where(incl, jnp.exp(jnp.where(incl, diff, 0.0)), 0.0)
            decay_strict = jnp.where(strict, decay_incl, 0.0)
            kb = k * beta
            e_gc = jnp.exp(gc_c)
            k_b = k.astype(BF16)
            qk = _dot_nt(jnp.concatenate([kb, q], axis=0).astype(BF16), k_b)
            lmat = qk[:C] * decay_strict
            attn = (qk[C:] * decay_incl).astype(BF16)
            t_b = _unit_lower_inverse(-lmat, 6).astype(BF16)
            rhs = jnp.concatenate([v * beta, kb * e_gc], axis=1).astype(BF16)
            sol = _dot(t_b, rhs)
            u = sol[:, :B_HEAD_DIM]
            w = sol[:, B_HEAD_DIM:].astype(BF16)
            q_dec = (q * e_gc).astype(BF16)
            k_dec_t = (k * jnp.exp(gc_last - gc_c)).T.astype(BF16)
            g_last = jnp.exp(gc_last)
            st = st_ref[d, hd]
            s_b = st.astype(BF16)
            ws_qs = _dot(jnp.concatenate([w, q_dec], axis=0), s_b)
            v_new = u - ws_qs[:C]
            vn_b = v_new.astype(BF16)
            o_ref[0, :, sl] = ws_qs[C:] + _dot(attn, vn_b)
            st_ref[d, hd] = st * g_last + _dot(k_dec_t, vn_b)


def _gdn_scan(q, k, v, gcol, grow, ctx_len):
    B, S, W = q.shape
    nc = S // CHUNK
    c0, c1 = _scan_chunk_maps(ctx_len // CHUNK, nc)
    tok = lambda cm: pl.BlockSpec((1, CHUNK, W), lambda b, s: (b, cm(s), 0))
    gcs = lambda cm: pl.BlockSpec((1, CHUNK, LANES), lambda b, s: (b, cm(s), 0))
    grs = lambda cm: pl.BlockSpec((1, 1, 4 * B_HEADS, CHUNK), lambda b, s: (b, cm(s), 0, 0))
    grow_c = grow.reshape(B, 4 * B_HEADS, nc, CHUNK).transpose(0, 2, 1, 3)
    o = jax.ShapeDtypeStruct((B, S, W), F32)
    return pl.pallas_call(
        _gdn_scan_kernel,
        grid=(B, nc),
        in_specs=[tok(c0), tok(c0), tok(c0), gcs(c0), grs(c0), tok(c1), tok(c1), tok(c1), gcs(c1), grs(c1)],
        out_specs=[tok(c0), tok(c1)],
        out_shape=[o, o],
        scratch_shapes=[pltpu.VMEM((2, B_HEADS, B_HEAD_DIM, B_HEAD_DIM), F32)],
        compiler_params=_cparams(("parallel", "arbitrary")),
        name="gdn_scan",
    )(q, k, v, gcol, grow_c, q, k, v, gcol, grow_c)


def _conf_kernel(val_ref, gate_ref, dw_ref, o_ref, ctx_scr, w_scr, h_scr, *, ctx_len, n_rows):
    cblk = pl.program_id(1)
    K = DW_CONV
    half = K // 2
    pad = 2 * SUBLANES
    stride = GRID_W + 2 * pad
    u = val_ref[0] * _sigmoid(gate_ref[0])
    dw = dw_ref[...]

    ctx_scr[...] = jnp.zeros_like(ctx_scr)
    ctx_scr[pad:pad + ctx_len, :] = u[:ctx_len]
    acc = jnp.zeros((ctx_len, LANES), F32)
    for j in range(K):
        acc = acc + dw[j:j + 1] * ctx_scr[pad - half + j:pad - half + j + ctx_len, :]
    o_ref[0, :ctx_len, :] = acc

    lat = u[ctx_len:]

    @pl.when(cblk < 2)
    def _():
        w_scr[...] = jnp.zeros_like(w_scr)
        for r in range(n_rows):
            w_scr[r * stride + pad:r * stride + pad + GRID_W, :] = lat[r * GRID_W:(r + 1) * GRID_W]
        for r in range(n_rows):
            a = jnp.zeros((GRID_W, LANES), F32)
            for j in range(K):
                o = r * stride + pad - half + j
                a = a + dw[j:j + 1] * w_scr[o:o + GRID_W, :]
            o_ref[0, ctx_len + r * GRID_W:ctx_len + (r + 1) * GRID_W, :] = a

    @pl.when(cblk >= 2)
    def _():
        hp = half * GRID_W
        h_scr[...] = jnp.zeros_like(h_scr)
        h_scr[hp:hp + n_rows * GRID_W, :] = lat
        for r in range(n_rows):
            a = jnp.zeros((GRID_W, LANES), F32)
            for j in range(K):
                rr = r + j - half
                if 0 <= rr < n_rows:
                    a = a + dw[j:j + 1] * h_scr[hp + rr * GRID_W:hp + (rr + 1) * GRID_W, :]
            o_ref[0, ctx_len + r * GRID_W:ctx_len + (r + 1) * GRID_W, :] = a


def _conformer_conv(P, dw, ctx_len):
    B, S, _ = P.shape
    W = BRANCH_WIDTH
    n_rows = (S - ctx_len) // GRID_W
    pad = 2 * SUBLANES
    nblk = W // LANES
    return pl.pallas_call(
        functools.partial(_conf_kernel, ctx_len=ctx_len, n_rows=n_rows),
        grid=(B, nblk),
        in_specs=[pl.BlockSpec((1, S, LANES), lambda b, c: (b, 0, COL_C // LANES + c)),
                  pl.BlockSpec((1, S, LANES), lambda b, c: (b, 0, (COL_C + W) // LANES + c)),
                  pl.BlockSpec((DW_CONV, LANES), lambda b, c: (0, c))],
        out_specs=pl.BlockSpec((1, S, LANES), lambda b, c: (b, 0, c)),
        out_shape=jax.ShapeDtypeStruct((B, S, W), F32),
        scratch_shapes=[pltpu.VMEM((ctx_len + 2 * pad, LANES), F32),
                        pltpu.VMEM((n_rows * (GRID_W + 2 * pad), LANES), F32),
                        pltpu.VMEM(((n_rows + 2 * (DW_CONV // 2)) * GRID_W, LANES), F32)],
        compiler_params=_cparams(("parallel", "parallel")),
        name="conformer_conv",
    )(P, P, dw)


def _merge_kernel(x_ref, m2b_ref, m2c_ref, y0_ref, y1_ref, bonus_ref, g_ref, o0_ref, o1_ref, z_ref,
                  yc_ref, pg0_ref, pg1_ref, pg2_ref, lng_ref, lnb_ref, gng_ref, dwb_ref, cg_ref, cb_ref,
                  wb_ref, wo_ref, out_ref, *, ctx_len):
    i = pl.program_id(1)
    W = BRANCH_WIDTH
    D = x_ref.shape[-1]
    n = x_ref.shape[1]
    y = y0_ref[0] + y1_ref[0]
    gsum = _group_ones(W, A_HEAD_DIM)
    mu = _dot_sel_right(y, gsum) * (1.0 / A_HEAD_DIM)
    yc = y - mu
    var = _dot_sel_right(yc * yc, gsum) * (1.0 / A_HEAD_DIM)
    ya = (yc * lax.rsqrt(var + LN_X_EPS) * lng_ref[...] + lnb_ref[...] + bonus_ref[0]) * g_ref[0]
    o = o0_ref[0] + o1_ref[0]
    z = z_ref[0]
    parts = []
    for hd in range(B_HEADS):
        oh = o[:, hd * B_HEAD_DIM:(hd + 1) * B_HEAD_DIM]
        parts.append(oh * lax.rsqrt(jnp.mean(oh * oh, axis=-1, keepdims=True) + EPS) * gng_ref[...])
    yb = jnp.concatenate(parts, axis=1) * _silu(z)
    c = yc_ref[0] + dwb_ref[...]
    cm = jnp.mean(c, axis=-1, keepdims=True)
    cc = c - cm
    cv = jnp.mean(cc * cc, axis=-1, keepdims=True)
    ycf = _silu(cc * lax.rsqrt(cv + EPS) * cg_ref[...] + cb_ref[...])

    acc = jnp.zeros((n, D), F32)
    for nb, (br, pg_ref) in enumerate(((ya, pg0_ref), (yb, pg1_ref), (ycf, pg2_ref))):
        up = _dot(br.astype(BF16), wb_ref[nb])
        acc = acc + _sigmoid(pg_ref[0]) * up
    mix = _dot(acc.astype(BF16), wo_ref[...])
    row = i * n + lax.broadcasted_iota(I32, (n, 1), 0)
    m2 = jnp.where(row < ctx_len, m2c_ref[0], m2b_ref[0])
    out_ref[0] = x_ref[0] + m2 * mix


def _merge(x, mods, ctx_row, y0, y1, bonus, g, o0, o1, P, ycv, ln_g, ln_b, gdn_g, dw_b, c_g, c_b, wb, wo,
           ctx_len):
    B, S, D = x.shape
    W = BRANCH_WIDTH
    nt = S // ROW_TILE
    assert COL_Z % W == 0 and COL_G % D == 0
    tokw = pl.BlockSpec((1, ROW_TILE, W), lambda b, i: (b, i, 0))
    tokd = pl.BlockSpec((1, ROW_TILE, D), lambda b, i: (b, i, 0))
    gate = lambda nb: pl.BlockSpec((1, ROW_TILE, D), lambda b, i: (b, i, COL_G // D + nb))
    full = lambda shape: pl.BlockSpec(shape, lambda b, i: tuple(0 for _ in shape))
    return pl.pallas_call(
        functools.partial(_merge_kernel, ctx_len=ctx_len),
        grid=(B, nt),
        in_specs=[tokd] + _mod_specs(2, ctx_row, (2,), D)
                 + [tokw, tokw, tokw, tokw, tokw, tokw,
                    pl.BlockSpec((1, ROW_TILE, W), lambda b, i: (b, i, COL_Z // W)),
                    tokw, gate(0), gate(1), gate(2),
                    full((1, W)), full((1, W)), full((1, B_HEAD_DIM)), full((1, W)), full((1, W)),
                    full((1, W)), full((3, W, D)), full((D, D))],
        out_specs=tokd,
        out_shape=jax.ShapeDtypeStruct((B, S, D), F32),
        input_output_aliases={0: 0},
        compiler_params=_cparams(("parallel", "parallel")),
        name="merge",
    )(x, mods, mods, y0, y1, bonus, g, o0, o1, P, ycv, P, P, P,
      ln_g.reshape(1, W), ln_b.reshape(1, W), gdn_g.reshape(1, B_HEAD_DIM), dw_b.reshape(1, W),
      c_g.reshape(1, W), c_b.reshape(1, W), wb, wo)


def _ffn_pre_kernel(x_ref, g_ref, shb_ref, scb_ref, shc_ref, scc_ref, wr_ref, rb_ref, h_ref, e_ref, w_ref,
                    *, ctx_len):
    i = pl.program_id(1)
    n = x_ref.shape[1]
    mods = (shb_ref[0], scb_ref[0], shc_ref[0], scc_ref[0])
    h = _modulated(x_ref[0], g_ref[...], mods, i * n, ctx_len)
    h_ref[0] = h
    logits = lax.dot_general(wr_ref[...], h, (((1,), (1,)), ((), ())),
                             preferred_element_type=F32, precision=HIGHEST)
    scores = _sigmoid(logits)
    sel = scores + rb_ref[...]
    rows = [sel[e:e + 1, :] for e in range(N_EXPERTS)]
    srow = [scores[e:e + 1, :] for e in range(N_EXPERTS)]
    gscore = []
    for gi in range(N_GROUPS):
        m = rows[gi * EXP_PER_GROUP:(gi + 1) * EXP_PER_GROUP]
        best = None
        for a in range(EXP_PER_GROUP):
            for b in range(a + 1, EXP_PER_GROUP):
                t = m[a] + m[b]
                best = t if best is None else jnp.maximum(best, t)
        gscore.append(best)
    grp = jnp.zeros_like(gscore[0], dtype=I32)
    gbest = gscore[0]
    for gi in range(1, N_GROUPS):
        better = gscore[gi] > gbest
        grp = jnp.where(better, gi, grp)
        gbest = jnp.where(better, gscore[gi], gbest)
    mem = []
    mem_s = []
    for a in range(EXP_PER_GROUP):
        va = rows[a]
        sa = srow[a]
        for gi in range(1, N_GROUPS):
            va = jnp.where(grp == gi, rows[gi * EXP_PER_GROUP + a], va)
            sa = jnp.where(grp == gi, srow[gi * EXP_PER_GROUP + a], sa)
        mem.append(va)
        mem_s.append(sa)
    neg = jnp.full_like(mem[0], -jnp.inf)
    i1 = jnp.zeros_like(grp)
    b1 = mem[0]
    for a in range(1, EXP_PER_GROUP):
        better = mem[a] > b1
        i1 = jnp.where(better, a, i1)
        b1 = jnp.where(better, mem[a], b1)
    i2 = jnp.zeros_like(grp)
    b2 = neg
    first = True
    for a in range(EXP_PER_GROUP):
        cand = jnp.where(i1 == a, neg, mem[a])
        if first:
            b2 = cand
            first = False
        else:
            better = cand > b2
            i2 = jnp.where(better, a, i2)
            b2 = jnp.where(better, cand, b2)
    s1 = mem_s[0]
    s2 = mem_s[0]
    for a in range(1, EXP_PER_GROUP):
        s1 = jnp.where(i1 == a, mem_s[a], s1)
        s2 = jnp.where(i2 == a, mem_s[a], s2)
    tot = s1 + s2
    e_ref[0, 0:1, :] = grp * EXP_PER_GROUP + i1
    e_ref[0, 1:2, :] = grp * EXP_PER_GROUP + i2
    w_ref[0, 0:1, :] = s1 / tot
    w_ref[0, 1:2, :] = s2 / tot


def _ffn_pre(x, gain, mods, ctx_row, w_router, router_bias, ctx_len):
    B, S, D = x.shape
    nt = S // ROW_TILE
    tokd = pl.BlockSpec((1, ROW_TILE, D), lambda b, i: (b, i, 0))
    full = lambda shape: pl.BlockSpec(shape, lambda b, i: tuple(0 for _ in shape))
    sel = pl.BlockSpec((1, 2, ROW_TILE), lambda b, i: (b * nt + i, 0, 0))
    return pl.pallas_call(
        functools.partial(_ffn_pre_kernel, ctx_len=ctx_len),
        grid=(B, nt),
        in_specs=[tokd, full((1, D))] + _mod_specs(2, ctx_row, (3, 4), D)
                 + [full((N_EXPERTS, D)), full((N_EXPERTS, 1))],
        out_specs=[tokd, sel, sel],
        out_shape=[jax.ShapeDtypeStruct((B, S, D), F32),
                   jax.ShapeDtypeStruct((B * nt, 2, ROW_TILE), I32),
                   jax.ShapeDtypeStruct((B * nt, 2, ROW_TILE), F32)],
        compiler_params=_cparams(("parallel", "parallel")),
        name="ffn_pre",
    )(x, gain.reshape(1, D), mods, mods, mods, mods, w_router.T, router_bias.reshape(N_EXPERTS, 1))


def _moe_dispatch(e, w, n_tok):
    R = MOE_ROWS
    A = 2 * n_tok
    ef = e.reshape(A)
    wf = w.reshape(A)
    onehot = (ef[:, None] == jnp.arange(N_EXPERTS, dtype=I32)[None, :]).astype(I32)
    csum = jnp.cumsum(onehot, axis=0)
    rank = jnp.take_along_axis(csum, ef[:, None], axis=1)[:, 0] - 1
    sizes = csum[-1]
    padded = (sizes + R - 1) // R * R
    pad_ends = jnp.cumsum(padded)
    pad_starts = pad_ends - padded
    dest = pad_starts[ef] + rank
    n_blocks = -(-A // R) + N_EXPERTS
    rows = n_blocks * R
    pair = jnp.arange(A, dtype=I32)
    tok = jnp.where(pair >= n_tok, pair - n_tok, pair)
    ridx = jnp.arange(rows, dtype=I32)
    row_src = jnp.zeros((rows,), I32).at[dest].set(tok)
    row_dst = (A + ridx % R).at[dest].set(pair)
    row_w = jnp.zeros((rows,), F32).at[dest].set(wf)
    block_e = jnp.minimum(jnp.searchsorted(pad_ends, jnp.arange(n_blocks, dtype=I32) * R, side='right'),
                          N_EXPERTS - 1).astype(I32)
    return row_src.reshape(n_blocks, 1, R), row_dst.reshape(n_blocks, 1, R), row_w.reshape(rows, 1), block_e


def _moe_kernel(be_ref, src_ref, dst_ref, rw_ref, h_hbm, wg_ref, wu_ref, wd_ref, y_hbm,
                xbuf, ybuf, sem_in, sem_out):
    del be_ref
    R = MOE_ROWS

    def row_in(r):
        return pltpu.make_async_copy(h_hbm.at[pl.ds(src_ref[0, 0, r], 1)], xbuf.at[pl.ds(r, 1)], sem_in)

    def row_out(r):
        return pltpu.make_async_copy(ybuf.at[pl.ds(r, 1)], y_hbm.at[pl.ds(dst_ref[0, 0, r], 1)], sem_out)

    def start_in(r, c):
        row_in(r).start()
        return c

    def wait_in(r, c):
        row_in(r).wait()
        return c

    lax.fori_loop(0, R, start_in, 0)
    lax.fori_loop(0, R, wait_in, 0)
    xb = xbuf[...].astype(BF16)
    gate = _dot(xb, wg_ref[0])
    up = _dot(xb, wu_ref[0])
    act = (_silu(gate) * up).astype(BF16)
    ybuf[...] = _dot(act, wd_ref[0]) * rw_ref[...]

    def start_out(r, c):
        row_out(r).start()
        return c

    def wait_out(r, c):
        row_out(r).wait()
        return c

    lax.fori_loop(0, R, start_out, 0)
    lax.fori_loop(0, R, wait_out, 0)


def _moe(h2, row_src, row_dst, row_w, block_e, wg, wu, wd):
    n_tok, D = h2.shape
    n_blocks = block_e.shape[0]
    R = MOE_ROWS
    F = wg.shape[-1]
    grid_spec = pltpu.PrefetchScalarGridSpec(
        num_scalar_prefetch=1,
        grid=(n_blocks,),
        in_specs=[pl.BlockSpec((1, 1, R), lambda b, be: (b, 0, 0), memory_space=pltpu.SMEM),
                  pl.BlockSpec((1, 1, R), lambda b, be: (b, 0, 0), memory_space=pltpu.SMEM),
                  pl.BlockSpec((R, 1), lambda b, be: (b, 0)),
                  pl.BlockSpec(memory_space=pl.ANY),
                  pl.BlockSpec((1, D, F), lambda b, be: (be[b], 0, 0)),
                  pl.BlockSpec((1, D, F), lambda b, be: (be[b], 0, 0)),
                  pl.BlockSpec((1, F, D), lambda b, be: (be[b], 0, 0))],
        out_specs=pl.BlockSpec(memory_space=pl.ANY),
        scratch_shapes=[pltpu.VMEM((R, D), F32), pltpu.VMEM((R, D), F32),
                        pltpu.SemaphoreType.DMA, pltpu.SemaphoreType.DMA],
    )
    return pl.pallas_call(
        _moe_kernel,
        grid_spec=grid_spec,
        out_shape=jax.ShapeDtypeStruct((2 * n_tok + R, D), F32),
        compiler_params=_cparams(("arbitrary",)),
        name="moe",
    )(block_e, row_src, row_dst, row_w, h2, wg, wu, wd)


def _ffn_add_kernel(x_ref, m5b_ref, m5c_ref, y0_ref, y1_ref, o_ref, *, ctx_len):
    i = pl.program_id(1)
    n = x_ref.shape[1]
    row = i * n + lax.broadcasted_iota(I32, (n, 1), 0)
    m5 = jnp.where(row < ctx_len, m5c_ref[0], m5b_ref[0])
    o_ref[0] = x_ref[0] + m5 * (y0_ref[...] + y1_ref[...])


def _ffn_add(x, mods, ctx_row, y, ctx_len):
    B, S, D = x.shape
    nt = S // ROW_TILE
    tokd = pl.BlockSpec((1, ROW_TILE, D), lambda b, i: (b, i, 0))
    yspec = lambda slot: pl.BlockSpec((ROW_TILE, D), lambda b, i: (slot * B * nt + b * nt + i, 0))
    return pl.pallas_call(
        functools.partial(_ffn_add_kernel, ctx_len=ctx_len),
        grid=(B, nt),
        in_specs=[tokd] + _mod_specs(2, ctx_row, (5,), D) + [yspec(0), yspec(1)],
        out_specs=tokd,
        out_shape=jax.ShapeDtypeStruct((B, S, D), F32),
        input_output_aliases={0: 0},
        compiler_params=_cparams(("parallel", "parallel")),
        name="ffn_add",
    )(x, mods, mods, y, y)


def _final_kernel(x_ref, m5_ref, y0_ref, y1_ref, g_ref, o_ref):
    x = x_ref[0] + m5_ref[0] * (y0_ref[...] + y1_ref[...])
    o_ref[0] = x * lax.rsqrt(jnp.mean(x * x, axis=-1, keepdims=True) + EPS) * g_ref[...]


def _final(x, mods, y, gain, ctx_len):
    B, S, D = x.shape
    nt = S // ROW_TILE
    skip = ctx_len // ROW_TILE
    tok_in = pl.BlockSpec((1, ROW_TILE, D), lambda b, i: (b, i + skip, 0))
    yspec = lambda slot: pl.BlockSpec((ROW_TILE, D), lambda b, i: (slot * B * nt + b * nt + i + skip, 0))
    return pl.pallas_call(
        _final_kernel,
        grid=(B, nt - skip),
        in_specs=[tok_in, pl.BlockSpec((1, 1, D), lambda b, i: (b, 0, 5)), yspec(0), yspec(1),
                  pl.BlockSpec((1, D), lambda b, i: (0, 0))],
        out_specs=pl.BlockSpec((1, ROW_TILE, D), lambda b, i: (b, i, 0)),
        out_shape=jax.ShapeDtypeStruct((B, S - ctx_len, D), F32),
        compiler_params=_cparams(("parallel", "parallel")),
        name="final_norm",
    )(x, mods, y, y, gain.reshape(1, D))


def _pad_w_in(w):
    W = BRANCH_WIDTH
    a = w[:, :A_COLS]
    qkvz = w[:, A_COLS:A_COLS + 4 * W]
    ab = w[:, A_COLS + 4 * W:A_COLS + 4 * W + 4 * B_HEADS]
    rest = w[:, A_COLS + 4 * W + 4 * B_HEADS:]
    ab = jnp.pad(ab, ((0, 0), (0, LANES - 4 * B_HEADS)))
    out = jnp.concatenate([a, ab, qkvz, rest], axis=1).astype(BF16)
    assert out.shape[1] == N_IN_PAD
    return out


def kernel(x, c, ctx, c_ctx, w_ada, b_ada, norm_mix, norm_ffn, norm_final, w_in, rwkv_mu, rwkv_w0, rwkv_w2, rwkv_a0, rwkv_a2, rwkv_g2, rwkv_kk, rwkv_ka, rwkv_rk, rwkv_ln_g, rwkv_ln_b, gdn_conv, gdn_A_log, gdn_dt_bias, gdn_norm, conf_dw, conf_dw_b, conf_ln_g, conf_ln_b, w_branch, w_out, w_router, router_bias, w_e_gate, w_e_up, w_e_down):
    B, T, D = x.shape
    ctx_len = ctx.shape[1]
    depth = w_in.shape[0]
    assert ctx_len == ROW_TILE and T % ROW_TILE == 0 and T % GRID_W == 0
    S = ctx_len + T
    n_tok = B * S

    xs = jnp.concatenate([ctx, x], axis=1)
    rows = -(-(B + 1) // SUBLANES) * SUBLANES
    svec = jnp.concatenate([c, c_ctx[None, :], jnp.zeros((rows - B - 1, D), F32)], axis=0)
    mods_all = _ada(svec, w_ada, b_ada)
    ctx_row = B

    out = None
    for l in range(depth):
        mods = mods_all[l].reshape(rows, 1, 6 * D)
        P = _inproj(xs, norm_mix[l], mods, ctx_row, _pad_w_in(w_in[l]), ctx_len)
        prep = _rwkv_prep(P, rwkv_mu[l], rwkv_w0[l], rwkv_w2[l], rwkv_a0[l], rwkv_a2[l], rwkv_g2[l],
                          rwkv_kk[l], rwkv_ka[l], rwkv_rk[l])
        bonus, gate = prep[11], prep[12]
        y0, y1 = _rwkv_scan(prep[:11], ctx_len)
        q, k, v, gcol, grow = _gdn_prep(P, gdn_conv[l], gdn_A_log[l], gdn_dt_bias[l])
        o0, o1 = _gdn_scan(q, k, v, gcol, grow, ctx_len)
        ycv = _conformer_conv(P, conf_dw[l], ctx_len)
        xs = _merge(xs, mods, ctx_row, y0, y1, bonus, gate, o0, o1, P, ycv,
                    rwkv_ln_g[l], rwkv_ln_b[l], gdn_norm[l], conf_dw_b[l], conf_ln_g[l], conf_ln_b[l],
                    w_branch[l].astype(BF16), w_out[l].astype(BF16), ctx_len)
        h2, e, w = _ffn_pre(xs, norm_ffn[l], mods, ctx_row, w_router, router_bias, ctx_len)
        e = e.transpose(1, 0, 2).reshape(2, n_tok)
        w = w.transpose(1, 0, 2).reshape(2, n_tok)
        row_src, row_dst, row_w, block_e = _moe_dispatch(e, w, n_tok)
        y = _moe(h2.reshape(n_tok, D), row_src, row_dst, row_w, block_e,
                 w_e_gate[l].astype(BF16), w_e_up[l].astype(BF16), w_e_down[l].astype(BF16))
        if l < depth - 1:
            xs = _ffn_add(xs, mods, ctx_row, y, ctx_len)
        else:
            out = _final(xs, mods, y, norm_final, ctx_len)
    return out
```

```python
import functools
import math

import jax
import jax.numpy as jnp
from jax import lax
from jax.experimental import pallas as pl
from jax.experimental.pallas import tpu as pltpu

F32 = jnp.float32
BF16 = jnp.bfloat16
I32 = jnp.int32
HIGHEST = lax.Precision.HIGHEST

EPS = 1e-6
LN_X_EPS = 64e-5
GRID_W = 64
BRANCH_WIDTH = 512
A_HEADS = 8
A_HEAD_DIM = 64
B_HEADS = 4
B_HEAD_DIM = 128
DECAY_LORA = 64
SHORT_CONV = 5
DW_CONV = 31
N_EXPERTS = 16
N_GROUPS = 4
EXP_PER_GROUP = 4
D_EXPERT = 512

ROW_TILE = 256
CHUNK = 64
MOE_ROWS = 128
LANES = 128
SUBLANES = 8
VMEM_LIMIT = 56 * 1024 * 1024

A_COLS = 1920
COL_A = 0
COL_AB = 1920
COL_Q = 2048
COL_Z = 3584
COL_C = 4096
COL_G = 5120
N_IN_PAD = 8192
LORA_W = 3 * LANES


def _cparams(sem):
    return pltpu.CompilerParams(dimension_semantics=sem, vmem_limit_bytes=VMEM_LIMIT)


def _sigmoid(x):
    return 1.0 / (1.0 + jnp.exp(-x))


def _silu(x):
    return x * _sigmoid(x)


def _softplus(x):
    return jnp.maximum(x, 0.0) + jnp.log(1.0 + jnp.exp(-jnp.abs(x)))


def _dot(a, b):
    return jnp.dot(a, b, preferred_element_type=F32)


def _dot_nt(a, b):
    return lax.dot_general(a, b, (((1,), (1,)), ((), ())), preferred_element_type=F32)


def _bf16_parts(x, n):
    parts = []
    r = x
    for _ in range(n):
        p = r.astype(BF16)
        parts.append(p)
        r = r - p.astype(F32)
    return parts


def _dot_sel_left(m, x, n=3):
    acc = None
    for p in _bf16_parts(x, n):
        t = _dot(m, p)
        acc = t if acc is None else acc + t
    return acc


def _dot_sel_right(x, m, n=2):
    acc = None
    for p in _bf16_parts(x, n):
        t = _dot(p, m)
        acc = t if acc is None else acc + t
    return acc


def _group_ones(n, group):
    r = lax.broadcasted_iota(I32, (n, n), 0) // group
    c = lax.broadcasted_iota(I32, (n, n), 1) // group
    return jnp.where(r == c, 1.0, 0.0).astype(BF16)


def _chunk_tri(n, lower):
    r = lax.broadcasted_iota(I32, (n, n), 0)
    c = lax.broadcasted_iota(I32, (n, n), 1)
    same = (r // CHUNK) == (c // CHUNK)
    keep = (c <= r) if lower else (c >= r)
    return jnp.where(same & keep, 1.0, 0.0).astype(BF16)


def _shift_down(x, prev_rows, s):
    n = x.shape[0]
    out = pltpu.roll(x, s, axis=0)
    row = lax.broadcasted_iota(I32, (n, 1), 0)
    for q in range(s):
        out = jnp.where(row == q, prev_rows[SUBLANES - s + q:SUBLANES - s + q + 1, :], out)
    return out


def _shift_up(x, next_rows, s):
    n = x.shape[0]
    out = pltpu.roll(x, n - s, axis=0)
    row = lax.broadcasted_iota(I32, (n, 1), 0)
    for q in range(s):
        out = jnp.where(row == n - s + q, next_rows[q:q + 1, :], out)
    return out


def _halo_rows(p_prev_ref, p_next_ref, i, n_tiles):
    prev_rows = p_prev_ref[0]
    next_rows = p_next_ref[0]
    prev_rows = jnp.where(i <= 1, 0.0, prev_rows)
    next_rows = jnp.where((i == 0) | (i == n_tiles - 1), 0.0, next_rows)
    return prev_rows, next_rows


def _halo_specs(width, col, tiles_per_batch):
    assert col % width == 0
    col_block = col // width
    sub_per_tile = ROW_TILE // SUBLANES
    last = tiles_per_batch * sub_per_tile - 1
    main = pl.BlockSpec((1, ROW_TILE, width), lambda b, i: (b, i, col_block))
    prev = pl.BlockSpec((1, SUBLANES, width),
                        lambda b, i: (b, jnp.maximum(i * sub_per_tile - 1, 0), col_block))
    nxt = pl.BlockSpec((1, SUBLANES, width),
                       lambda b, i: (b, jnp.minimum((i + 1) * sub_per_tile, last), col_block))
    return [main, prev, nxt]


def _ada_kernel(s_ref, w_ref, b_ref, o_ref):
    s = s_ref[...]
    o_ref[0] = jnp.dot(_silu(s), w_ref[0], preferred_element_type=F32, precision=HIGHEST) + b_ref[0]


def _ada(svec, w_ada, b_ada):
    L, D, N = w_ada.shape
    rows = svec.shape[0]
    tn = 1536
    return pl.pallas_call(
        _ada_kernel,
        grid=(L, N // tn),
        in_specs=[pl.BlockSpec((rows, D), lambda l, j: (0, 0)),
                  pl.BlockSpec((1, D, tn), lambda l, j: (l, 0, j)),
                  pl.BlockSpec((1, 1, tn), lambda l, j: (l, 0, j))],
        out_specs=pl.BlockSpec((1, rows, tn), lambda l, j: (l, 0, j)),
        out_shape=jax.ShapeDtypeStruct((L, rows, N), F32),
        compiler_params=_cparams(("parallel", "parallel")),
        name="ada",
    )(svec, w_ada, b_ada.reshape(L, 1, N))


def _modulated(x, gain, mods, row0, ctx_len):
    sh_b, sc_b, sh_c, sc_c = mods
    y = x * lax.rsqrt(jnp.mean(x * x, axis=-1, keepdims=True) + EPS) * gain
    row = row0 + lax.broadcasted_iota(I32, (x.shape[0], 1), 0)
    isc = row < ctx_len
    shift = jnp.where(isc, sh_c, sh_b)
    scale = jnp.where(isc, sc_c, sc_b)
    return y * (1.0 + scale) + shift


def _mod_specs(n_grid, ctx_row, chunks, d):
    def spec(row_of, c):
        if n_grid == 2:
            return pl.BlockSpec((1, 1, d), lambda b, i: (row_of(b), 0, c))
        return pl.BlockSpec((1, 1, d), lambda b, i, j: (row_of(b), 0, c))
    return ([spec(lambda b: b, c) for c in chunks] + [spec(lambda b: ctx_row, c) for c in chunks])


def _inproj_kernel(x_ref, g_ref, shb_ref, scb_ref, shc_ref, scc_ref, w_ref, o_ref, h_scr, *, tm, ctx_len):
    i = pl.program_id(1)
    j = pl.program_id(2)

    @pl.when(j == 0)
    def _():
        mods = (shb_ref[0], scb_ref[0], shc_ref[0], scc_ref[0])
        h_scr[...] = _modulated(x_ref[0], g_ref[...], mods, i * tm, ctx_len).astype(BF16)

    o_ref[0] = _dot(h_scr[...], w_ref[...])


def _inproj(x, gain, mods, ctx_row, wp, ctx_len):
    B, S, D = x.shape
    N = wp.shape[1]
    tm = S // 2
    tn = 1024
    return pl.pallas_call(
        functools.partial(_inproj_kernel, tm=tm, ctx_len=ctx_len),
        grid=(B, S // tm, N // tn),
        in_specs=[pl.BlockSpec((1, tm, D), lambda b, i, j: (b, i, 0)),
                  pl.BlockSpec((1, D), lambda b, i, j: (0, 0))]
                 + _mod_specs(3, ctx_row, (0, 1), D)
                 + [pl.BlockSpec((D, tn), lambda b, i, j: (0, j))],
        out_specs=pl.BlockSpec((1, tm, tn), lambda b, i, j: (b, i, j)),
        out_shape=jax.ShapeDtypeStruct((B, S, N), F32),
        scratch_shapes=[pltpu.VMEM((tm, D), BF16)],
        compiler_params=_cparams(("parallel", "parallel", "arbitrary")),
        name="inproj",
    )(x, gain.reshape(1, D), mods, mods, mods, mods, wp)


def _rwkv_prep_kernel(r_ref, rp_ref, rn_ref, k_ref, kp_ref_, kn_ref, vv_ref, vp_ref, vn_ref,
                      x_ref, xp_ref, xn_ref,
                      mu_ref, w0_ref, w2_ref, a0_ref, a2_ref, g2_ref, kk_ref, ka_ref, rk_ref,
                      rt0_ref, kp0_ref, bt0_ref, kt0_ref, pc0_ref,
                      rt1_ref, kp1_ref, bt1_ref, kt1_ref, pc1_ref,
                      v_ref, bonus_ref, g_ref, *, n_tiles):
    i = pl.program_id(1)
    W = BRANCH_WIDTH

    def shifted(main, prv, nxt, col, width):
        p = main[0]
        prev_rows, next_rows = _halo_rows(prv, nxt, i, n_tiles)
        mu = mu_ref[:, col:col + width]
        return (p + mu[0:1] * (_shift_down(p, prev_rows, 1) - p)
                + mu[1:2] * (_shift_up(p, next_rows, 1) - p))

    r = shifted(r_ref, rp_ref, rn_ref, 0, W)
    k = shifted(k_ref, kp_ref_, kn_ref, W, W)
    v = shifted(vv_ref, vp_ref, vn_ref, 2 * W, W)
    lora = shifted(x_ref, xp_ref, xn_ref, 3 * W, LORA_W)
    xw = jnp.tanh(lora[:, 0:LANES])
    xa = lora[:, LANES:2 * LANES]
    xg = _sigmoid(lora[:, 2 * LANES:3 * LANES])
    lane_hi = lax.broadcasted_iota(I32, xw.shape, 1) >= DECAY_LORA

    g_ref[0] = _dot(xg.astype(BF16), g2_ref[...].astype(BF16))
    v_ref[0] = v.astype(BF16)

    gsum = _group_ones(W, A_HEAD_DIM)
    kx = k * kk_ref[...]
    kk = kx * lax.rsqrt(_dot_sel_right(kx * kx, gsum) + EPS)
    bonus_ref[0] = _dot_sel_right(r * k * rk_ref[...], gsum) * v

    w2 = w2_ref[...].astype(BF16)
    a2 = a2_ref[...].astype(BF16)
    n = r.shape[0]
    outs = ((rt0_ref, kp0_ref, bt0_ref, kt0_ref, pc0_ref), (rt1_ref, kp1_ref, bt1_ref, kt1_ref, pc1_ref))
    for d in range(2):
        keep = lane_hi if d == 1 else jnp.logical_not(lane_hi)
        lw = w0_ref[d:d + 1, :] + _dot(jnp.where(keep, xw, 0.0).astype(BF16), w2)
        ew = _sigmoid(lw) * math.exp(-0.5)
        a = _sigmoid(a0_ref[d:d + 1, :] + _dot(jnp.where(keep, xa, 0.0).astype(BF16), a2))
        keff = k * (1.0 + (a - 1.0) * ka_ref[...])
        cs = _dot_sel_left(_chunk_tri(n, lower=(d == 0)), ew)
        p_incl = jnp.exp(-cs)
        p_excl = jnp.exp(ew - cs)
        p_inv = jnp.exp(cs)
        rt_ref, kp_ref, bt_ref, kt_ref, pc_ref = outs[d]
        rt_ref[0] = (r * p_incl).astype(BF16)
        kp_ref[0] = (kk * p_excl).astype(BF16)
        bt_ref[0] = (kk * a * p_inv).astype(BF16)
        kt_ref[0] = (keff * p_inv).astype(BF16)
        for c in range(n // CHUNK):
            row = c * CHUNK + (CHUNK - 1 if d == 0 else 0)
            pc_ref[0, c] = p_incl[row:row + 1, :]


def _rwkv_prep(P, mu, w0, w2, a0, a2, g2, k_k, k_a, r_k):
    B, S, _ = P.shape
    W = BRANCH_WIDTH
    nt = S // ROW_TILE
    cpt = ROW_TILE // CHUNK
    halo = (_halo_specs(W, COL_A, nt) + _halo_specs(W, COL_A + W, nt) + _halo_specs(W, COL_A + 2 * W, nt)
            + _halo_specs(LORA_W, COL_A + 3 * W, nt))
    full = lambda shape: pl.BlockSpec(shape, lambda b, i: tuple(0 for _ in shape))
    tok = pl.BlockSpec((1, ROW_TILE, W), lambda b, i: (b, i, 0))
    pcs = pl.BlockSpec((1, cpt, 1, W), lambda b, i: (b, i, 0, 0))
    bf = jax.ShapeDtypeStruct((B, S, W), BF16)
    f32 = jax.ShapeDtypeStruct((B, S, W), F32)
    pcshape = jax.ShapeDtypeStruct((B, S // CHUNK, 1, W), F32)
    return pl.pallas_call(
        functools.partial(_rwkv_prep_kernel, n_tiles=nt),
        grid=(B, nt),
        in_specs=halo + [full((2, A_COLS)), full((2, W)), full((2 * DECAY_LORA, W)), full((2, W)),
                         full((2 * DECAY_LORA, W)), full((LANES, W)), full((1, W)), full((1, W)),
                         full((1, W))],
        out_specs=[tok, tok, tok, tok, pcs, tok, tok, tok, tok, pcs, tok, tok, tok],
        out_shape=[bf, bf, bf, bf, pcshape, bf, bf, bf, bf, pcshape, bf, f32, f32],
        compiler_params=_cparams(("parallel", "parallel")),
        name="rwkv_prep",
    )(*([P] * 12), mu, w0, w2.reshape(2 * DECAY_LORA, W), a0, a2.reshape(2 * DECAY_LORA, W), g2,
      k_k.reshape(1, W), k_a.reshape(1, W), r_k.reshape(1, W))


def _unit_lower_inverse(a_neg, steps):
    n = a_neg.shape[0]
    eye = jnp.where(lax.broadcasted_iota(I32, (n, n), 0) == lax.broadcasted_iota(I32, (n, n), 1), 1.0, 0.0)
    t = eye + a_neg
    xp = a_neg.astype(BF16)
    for _ in range(steps - 1):
        x2 = _dot(xp, xp)
        xp = x2.astype(BF16)
        t = t + _dot(t.astype(BF16), xp)
    return t


def _stack_heads(x, lane_hi):
    z = jnp.zeros_like(x)
    return jnp.concatenate([jnp.where(lane_hi, z, x), jnp.where(lane_hi, x, z)], axis=0)


def _rwkv_scan_kernel(rt0, kp0, bt0, kt0, v0, pc0, rt1, kp1, bt1, kt1, v1, pc1, y0_ref, y1_ref, st_ref):
    s = pl.program_id(1)

    @pl.when(s == 0)
    def _():
        st_ref[...] = jnp.zeros_like(st_ref)

    C = CHUNK
    n2 = 2 * C
    ri = lax.broadcasted_iota(I32, (n2, n2), 0)
    ci = lax.broadcasted_iota(I32, (n2, n2), 1)
    same = (ri // C) == (ci // C)
    tr = ri % C
    ts = ci % C
    lane_hi = lax.broadcasted_iota(I32, (C, n2), 1) >= C
    eye = jnp.where(ri == ci, 1.0, 0.0)
    dirs = ((rt0, kp0, bt0, kt0, v0, pc0, y0_ref), (rt1, kp1, bt1, kt1, v1, pc1, y1_ref))

    def unit(d, j):
        rt, kp, bt, kt, vv, pc, y_ref = dirs[d]
        strict = same & ((ts < tr) if d == 0 else (ts > tr))
        incl = same & ((ts <= tr) if d == 0 else (ts >= tr))
        sl = slice(n2 * j, n2 * (j + 1))
        kap = kp[0, :, sl]
        r_ = rt[0, :, sl]
        b_ = bt[0, :, sl]
        k_ = kt[0, :, sl]
        v_st = _stack_heads(vv[0, :, sl], lane_hi)
        h = st_ref[d, j]
        krh = _dot(jnp.concatenate([kap, r_], axis=0), h.astype(BF16))
        aa = _dot_nt(jnp.concatenate([_stack_heads(kap, lane_hi), _stack_heads(r_, lane_hi)], axis=0),
                     jnp.concatenate([b_, b_, k_, k_], axis=0))
        yield
        x = jnp.where(strict, -aa[:n2, :n2], 0.0)
        a_ak = jnp.where(strict, aa[:n2, n2:], 0.0).astype(BF16)
        a_rb = jnp.where(incl, aa[n2:, :n2], 0.0).astype(BF16)
        a_rk = jnp.where(incl, aa[n2:, n2:], 0.0).astype(BF16)
        t = eye + x
        xp = x.astype(BF16)
        b_t = _stack_heads(b_, lane_hi).astype(F32).T.astype(BF16)
        k_t = _stack_heads(k_, lane_hi).astype(F32).T.astype(BF16)
        sv = _dot(jnp.concatenate([a_ak, a_rk, k_t], axis=0), v_st)
        akv = sv[:n2]
        arkv = sv[n2:2 * n2]
        kv0 = sv[2 * n2:]
        x2 = _dot(xp, xp)
        yield
        for _ in range(4):
            xp = x2.astype(BF16)
            tx = _dot(jnp.concatenate([t.astype(BF16), xp], axis=0), xp)
            yield
            t = t + tx[:n2]
            x2 = tx[n2:]
        t = t + _dot(t.astype(BF16), x2.astype(BF16))
        yield
        rhs = akv + _stack_heads(krh[:C], lane_hi)
        u_st = -_dot(t.astype(BF16), rhs.astype(BF16))
        yield
        yd = _dot(jnp.concatenate([a_rb, b_t], axis=0), u_st.astype(BF16))
        yield
        y_st = yd[:n2] + arkv
        y_ref[0, :, sl] = y_st[:C] + y_st[C:] + krh[C:]
        pcm = jnp.broadcast_to(pc[0, 0][:, sl], (n2, n2)).T
        st_ref[d, j] = pcm * (h + kv0 + yd[n2:])

    chains = [unit(d, j) for d in range(2) for j in range(BRANCH_WIDTH // n2)]
    while chains:
        alive = []
        for g in chains:
            try:
                next(g)
                alive.append(g)
            except StopIteration:
                pass
        chains = alive


def _scan_chunk_maps(nc_ctx, nc):
    def c0(s):
        return s

    def c1(s):
        return jnp.where(s < nc_ctx, nc_ctx - 1 - s, nc + nc_ctx - 1 - s)

    return c0, c1


def _rwkv_scan(prep, ctx_len):
    rt0, kp0, bt0, kt0, pc0, rt1, kp1, bt1, kt1, pc1, v = prep
    B, S, W = v.shape
    nc = S // CHUNK
    c0, c1 = _scan_chunk_maps(ctx_len // CHUNK, nc)
    tok = lambda cm: pl.BlockSpec((1, CHUNK, W), lambda b, s: (b, cm(s), 0))
    pcs = lambda cm: pl.BlockSpec((1, 1, 1, W), lambda b, s: (b, cm(s), 0, 0))
    y = jax.ShapeDtypeStruct((B, S, W), F32)
    return pl.pallas_call(
        _rwkv_scan_kernel,
        grid=(B, nc),
        in_specs=[tok(c0), tok(c0), tok(c0), tok(c0), tok(c0), pcs(c0),
                  tok(c1), tok(c1), tok(c1), tok(c1), tok(c1), pcs(c1)],
        out_specs=[tok(c0), tok(c1)],
        out_shape=[y, y],
        scratch_shapes=[pltpu.VMEM((2, W // (2 * CHUNK), 2 * CHUNK, 2 * CHUNK), F32)],
        compiler_params=_cparams(("parallel", "arbitrary")),
        name="rwkv_scan",
    )(rt0, kp0, bt0, kt0, v, pc0, rt1, kp1, bt1, kt1, v, pc1)


def _gdn_prep_kernel(qi_ref, qp_ref, qn_ref, ki_ref, kp_ref, kn_ref, vi_ref, vp_ref, vn_ref,
                     ab_ref, cw_ref, al_ref, dt_ref,
                     q_ref, k_ref, v_ref, gcol_ref, grow_ref, *, n_tiles):
    i = pl.program_id(1)
    W = BRANCH_WIDTH
    half = SHORT_CONV // 2

    def conv_silu(main, prv, nxt, col):
        x = main[0]
        prev_rows, next_rows = _halo_rows(prv, nxt, i, n_tiles)
        cw = cw_ref[:, col:col + W]
        acc = cw[half:half + 1] * x
        for s in range(1, half + 1):
            acc = acc + cw[half - s:half - s + 1] * _shift_down(x, prev_rows, s)
            acc = acc + cw[half + s:half + s + 1] * _shift_up(x, next_rows, s)
        return _silu(acc)

    yq = conv_silu(qi_ref, qp_ref, qn_ref, 0)
    yk = conv_silu(ki_ref, kp_ref, kn_ref, W)
    v_ref[0] = conv_silu(vi_ref, vp_ref, vn_ref, 2 * W)
    for hd in range(B_HEADS):
        sl = slice(hd * B_HEAD_DIM, (hd + 1) * B_HEAD_DIM)
        q = yq[:, sl]
        q_ref[0, :, sl] = q * lax.rsqrt(jnp.sum(q * q, axis=-1, keepdims=True) + EPS) * (B_HEAD_DIM ** -0.5)
        k = yk[:, sl]
        k_ref[0, :, sl] = k * lax.rsqrt(jnp.sum(k * k, axis=-1, keepdims=True) + EPS)

    ab = ab_ref[0]
    n = ab.shape[0]
    lane = lax.broadcasted_iota(I32, ab.shape, 1)
    g = -jnp.exp(al_ref[...]) * _softplus(ab + dt_ref[...])
    g = jnp.where(lane < 2 * B_HEADS, g, 0.0)
    gc_f = _dot_sel_left(_chunk_tri(n, lower=True), g)
    gc_b = _dot_sel_left(_chunk_tri(n, lower=False), g)
    gc = jnp.where(lane < B_HEADS, gc_f, gc_b)
    beta = _sigmoid(ab)
    out = jnp.where(lane < 2 * B_HEADS, gc, jnp.where(lane < 4 * B_HEADS, beta, 0.0))
    gcol_ref[0] = out
    grow_ref[0] = out.T[:4 * B_HEADS, :]


def _gdn_prep(P, conv_w, A_log, dt_bias):
    B, S, _ = P.shape
    W = BRANCH_WIDTH
    nt = S // ROW_TILE
    halo = _halo_specs(W, COL_Q, nt) + _halo_specs(W, COL_Q + W, nt) + _halo_specs(W, COL_Q + 2 * W, nt)
    pad = jnp.zeros((1, LANES - 2 * B_HEADS), F32)
    al = jnp.concatenate([A_log.reshape(1, 2 * B_HEADS), pad], axis=1)
    dt = jnp.concatenate([dt_bias.reshape(1, 2 * B_HEADS), pad], axis=1)
    full = lambda shape: pl.BlockSpec(shape, lambda b, i: tuple(0 for _ in shape))
    tok = pl.BlockSpec((1, ROW_TILE, W), lambda b, i: (b, i, 0))
    f32 = jax.ShapeDtypeStruct((B, S, W), F32)
    return pl.pallas_call(
        functools.partial(_gdn_prep_kernel, n_tiles=nt),
        grid=(B, nt),
        in_specs=halo + [pl.BlockSpec((1, ROW_TILE, LANES), lambda b, i: (b, i, COL_AB // LANES)),
                         full((SHORT_CONV, 3 * W)), full((1, LANES)), full((1, LANES))],
        out_specs=[tok, tok, tok,
                   pl.BlockSpec((1, ROW_TILE, LANES), lambda b, i: (b, i, 0)),
                   pl.BlockSpec((1, 4 * B_HEADS, ROW_TILE), lambda b, i: (b, 0, i))],
        out_shape=[f32, f32, f32, jax.ShapeDtypeStruct((B, S, LANES), F32),
                   jax.ShapeDtypeStruct((B, 4 * B_HEADS, S), F32)],
        compiler_params=_cparams(("parallel", "parallel")),
        name="gdn_prep",
    )(*([P] * 10), conv_w, al, dt)


def _gdn_scan_kernel(q0, k0, v0, gc0, gr0, q1, k1, v1, gc1, gr1, o0_ref, o1_ref, st_ref):
    s = pl.program_id(1)

    @pl.when(s == 0)
    def _():
        st_ref[...] = jnp.zeros_like(st_ref)

    C = CHUNK
    ri = lax.broadcasted_iota(I32, (C, C), 0)
    ci = lax.broadcasted_iota(I32, (C, C), 1)
    dirs = ((q0, k0, v0, gc0, gr0, o0_ref), (q1, k1, v1, gc1, gr1, o1_ref))
    eye = jnp.where(ri == ci, 1.0, 0.0)

    def unit(d, hd):
        qr, kr, vr, gcr, grr, o_ref = dirs[d]
        incl = (ci <= ri) if d == 0 else (ci >= ri)
        strict = (ci < ri) if d == 0 else (ci > ri)
        last = C - 1 if d == 0 else 0
        gcol = gcr[0]
        grow = grr[0, 0]
        sl = slice(hd * B_HEAD_DIM, (hd + 1) * B_HEAD_DIM)
        idx = d * B_HEADS + hd
        q = qr[0, :, sl]
        k = kr[0, :, sl]
        v = vr[0, :, sl]
        gc_c = gcol[:, idx:idx + 1]
        gc_r = grow[idx:idx + 1, :]
        beta = gcol[:, 2 * B_HEADS + idx:2 * B_HEADS + idx + 1]
        gc_last = gc_r[:, last:last + 1]
        diff = gc_c - gc_r
        decay_incl = jnp.where(incl, jnp.exp(jnp.where(incl, diff, 0.0)), 0.0)
        decay_strict = jnp.where(strict, decay_incl, 0.0)
        kb = k * beta
        e_gc = jnp.exp(gc_c)
        qk = _dot_nt(jnp.concatenate([kb, q], axis=0).astype(BF16), k.astype(BF16))
        yield
        x = -(qk[:C] * decay_strict)
        attn = (qk[C:] * decay_incl).astype(BF16)
        rhs = jnp.concatenate([v * beta, kb * e_gc], axis=1).astype(BF16)
        q_dec = (q * e_gc).astype(BF16)
        k_dec_t = (k * jnp.exp(gc_last - gc_c)).T.astype(BF16)
        g_last = jnp.exp(gc_last)
        t = eye + x
        xp = x.astype(BF16)
        x2 = _dot(xp, xp)
        yield
        for _ in range(4):
            xp = x2.astype(BF16)
            tx = _dot(jnp.concatenate([t.astype(BF16), xp], axis=0), xp)
            yield
            t = t + tx[:C]
            x2 = tx[C:]
        t = t + _dot(t.astype(BF16), x2.astype(BF16))
        yield
        sol = _dot(t.astype(BF16), rhs)
        yield
        u = sol[:, :B_HEAD_DIM]
        w = sol[:, B_HEAD_DIM:].astype(BF16)
        st = st_ref[d, hd]
        ws_qs = _dot(jnp.concatenate([w, q_dec], axis=0), st.astype(BF16))
        yield
        vn_b = (u - ws_qs[:C]).astype(BF16)
        od = _dot(jnp.concatenate([attn, k_dec_t], axis=0), vn_b)
        yield
        o_ref[0, :, sl] = ws_qs[C:] + od[:C]
        st_ref[d, hd] = st * g_last + od[C:]

    chains = [unit(d, hd) for d in range(2) for hd in range(B_HEADS)]
    while chains:
        alive = []
        for g in chains:
            try:
                next(g)
                alive.append(g)
            except StopIteration:
                pass
        chains = alive


def _gdn_scan(q, k, v, gcol, grow, ctx_len):
    B, S, W = q.shape
    nc = S // CHUNK
    c0, c1 = _scan_chunk_maps(ctx_len // CHUNK, nc)
    tok = lambda cm: pl.BlockSpec((1, CHUNK, W), lambda b, s: (b, cm(s), 0))
    gcs = lambda cm: pl.BlockSpec((1, CHUNK, LANES), lambda b, s: (b, cm(s), 0))
    grs = lambda cm: pl.BlockSpec((1, 1, 4 * B_HEADS, CHUNK), lambda b, s: (b, cm(s), 0, 0))
    grow_c = grow.reshape(B, 4 * B_HEADS, nc, CHUNK).transpose(0, 2, 1, 3)
    o = jax.ShapeDtypeStruct((B, S, W), F32)
    return pl.pallas_call(
        _gdn_scan_kernel,
        grid=(B, nc),
        in_specs=[tok(c0), tok(c0), tok(c0), gcs(c0), grs(c0), tok(c1), tok(c1), tok(c1), gcs(c1), grs(c1)],
        out_specs=[tok(c0), tok(c1)],
        out_shape=[o, o],
        scratch_shapes=[pltpu.VMEM((2, B_HEADS, B_HEAD_DIM, B_HEAD_DIM), F32)],
        compiler_params=_cparams(("parallel", "arbitrary")),
        name="gdn_scan",
    )(q, k, v, gcol, grow_c, q, k, v, gcol, grow_c)


def _conf_kernel(val_ref, gate_ref, dw_ref, o_ref, ctx_scr, w_scr, h_scr, *, ctx_len, n_rows):
    cblk = pl.program_id(1)
    K = DW_CONV
    half = K // 2
    pad = 2 * SUBLANES
    stride = GRID_W + 2 * pad
    u = val_ref[0] * _sigmoid(gate_ref[0])
    dw = dw_ref[...]

    ctx_scr[...] = jnp.zeros_like(ctx_scr)
    ctx_scr[pad:pad + ctx_len, :] = u[:ctx_len]
    acc = jnp.zeros((ctx_len, LANES), F32)
    for j in range(K):
        acc = acc + dw[j:j + 1] * ctx_scr[pad - half + j:pad - half + j + ctx_len, :]
    o_ref[0, :ctx_len, :] = acc

    lat = u[ctx_len:]

    @pl.when(cblk < 2)
    def _():
        w_scr[...] = jnp.zeros_like(w_scr)
        for r in range(n_rows):
            w_scr[r * stride + pad:r * stride + pad + GRID_W, :] = lat[r * GRID_W:(r + 1) * GRID_W]
        for r in range(n_rows):
            a = jnp.zeros((GRID_W, LANES), F32)
            for j in range(K):
                o = r * stride + pad - half + j
                a = a + dw[j:j + 1] * w_scr[o:o + GRID_W, :]
            o_ref[0, ctx_len + r * GRID_W:ctx_len + (r + 1) * GRID_W, :] = a

    @pl.when(cblk >= 2)
    def _():
        hp = half * GRID_W
        h_scr[...] = jnp.zeros_like(h_scr)
        h_scr[hp:hp + n_rows * GRID_W, :] = lat
        for r in range(n_rows):
            a = jnp.zeros((GRID_W, LANES), F32)
            for j in range(K):
                rr = r + j - half
                if 0 <= rr < n_rows:
                    a = a + dw[j:j + 1] * h_scr[hp + rr * GRID_W:hp + (rr + 1) * GRID_W, :]
            o_ref[0, ctx_len + r * GRID_W:ctx_len + (r + 1) * GRID_W, :] = a


def _conformer_conv(P, dw, ctx_len):
    B, S, _ = P.shape
    W = BRANCH_WIDTH
    n_rows = (S - ctx_len) // GRID_W
    pad = 2 * SUBLANES
    nblk = W // LANES
    return pl.pallas_call(
        functools.partial(_conf_kernel, ctx_len=ctx_len, n_rows=n_rows),
        grid=(B, nblk),
        in_specs=[pl.BlockSpec((1, S, LANES), lambda b, c: (b, 0, COL_C // LANES + c)),
                  pl.BlockSpec((1, S, LANES), lambda b, c: (b, 0, (COL_C + W) // LANES + c)),
                  pl.BlockSpec((DW_CONV, LANES), lambda b, c: (0, c))],
        out_specs=pl.BlockSpec((1, S, LANES), lambda b, c: (b, 0, c)),
        out_shape=jax.ShapeDtypeStruct((B, S, W), F32),
        scratch_shapes=[pltpu.VMEM((ctx_len + 2 * pad, LANES), F32),
                        pltpu.VMEM((n_rows * (GRID_W + 2 * pad), LANES), F32),
                        pltpu.VMEM(((n_rows + 2 * (DW_CONV // 2)) * GRID_W, LANES), F32)],
        compiler_params=_cparams(("parallel", "parallel")),
        name="conformer_conv",
    )(P, P, dw)


def _merge_kernel(x_ref, m2b_ref, m2c_ref, y0_ref, y1_ref, bonus_ref, g_ref, o0_ref, o1_ref, z_ref,
                  yc_ref, pg0_ref, pg1_ref, pg2_ref, lng_ref, lnb_ref, gng_ref, dwb_ref, cg_ref, cb_ref,
                  wb_ref, wo_ref, out_ref, *, ctx_len):
    i = pl.program_id(1)
    W = BRANCH_WIDTH
    D = x_ref.shape[-1]
    n = x_ref.shape[1]
    y = y0_ref[0] + y1_ref[0]
    gsum = _group_ones(W, A_HEAD_DIM)
    mu = _dot_sel_right(y, gsum) * (1.0 / A_HEAD_DIM)
    yc = y - mu
    var = _dot_sel_right(yc * yc, gsum) * (1.0 / A_HEAD_DIM)
    ya = (yc * lax.rsqrt(var + LN_X_EPS) * lng_ref[...] + lnb_ref[...] + bonus_ref[0]) * g_ref[0]
    o = o0_ref[0] + o1_ref[0]
    z = z_ref[0]
    parts = []
    for hd in range(B_HEADS):
        oh = o[:, hd * B_HEAD_DIM:(hd + 1) * B_HEAD_DIM]
        parts.append(oh * lax.rsqrt(jnp.mean(oh * oh, axis=-1, keepdims=True) + EPS) * gng_ref[...])
    yb = jnp.concatenate(parts, axis=1) * _silu(z)
    c = yc_ref[0] + dwb_ref[...]
    cm = jnp.mean(c, axis=-1, keepdims=True)
    cc = c - cm
    cv = jnp.mean(cc * cc, axis=-1, keepdims=True)
    ycf = _silu(cc * lax.rsqrt(cv + EPS) * cg_ref[...] + cb_ref[...])

    acc = jnp.zeros((n, D), F32)
    for nb, (br, pg_ref) in enumerate(((ya, pg0_ref), (yb, pg1_ref), (ycf, pg2_ref))):
        up = _dot(br.astype(BF16), wb_ref[nb])
        acc = acc + _sigmoid(pg_ref[0]) * up
    mix = _dot(acc.astype(BF16), wo_ref[...])
    row = i * n + lax.broadcasted_iota(I32, (n, 1), 0)
    m2 = jnp.where(row < ctx_len, m2c_ref[0], m2b_ref[0])
    out_ref[0] = x_ref[0] + m2 * mix


def _merge(x, mods, ctx_row, y0, y1, bonus, g, o0, o1, P, ycv, ln_g, ln_b, gdn_g, dw_b, c_g, c_b, wb, wo,
           ctx_len):
    B, S, D = x.shape
    W = BRANCH_WIDTH
    nt = S // ROW_TILE
    assert COL_Z % W == 0 and COL_G % D == 0
    tokw = pl.BlockSpec((1, ROW_TILE, W), lambda b, i: (b, i, 0))
    tokd = pl.BlockSpec((1, ROW_TILE, D), lambda b, i: (b, i, 0))
    gate = lambda nb: pl.BlockSpec((1, ROW_TILE, D), lambda b, i: (b, i, COL_G // D + nb))
    full = lambda shape: pl.BlockSpec(shape, lambda b, i: tuple(0 for _ in shape))
    return pl.pallas_call(
        functools.partial(_merge_kernel, ctx_len=ctx_len),
        grid=(B, nt),
        in_specs=[tokd] + _mod_specs(2, ctx_row, (2,), D)
                 + [tokw, tokw, tokw, tokw, tokw, tokw,
                    pl.BlockSpec((1, ROW_TILE, W), lambda b, i: (b, i, COL_Z // W)),
                    tokw, gate(0), gate(1), gate(2),
                    full((1, W)), full((1, W)), full((1, B_HEAD_DIM)), full((1, W)), full((1, W)),
                    full((1, W)), full((3, W, D)), full((D, D))],
        out_specs=tokd,
        out_shape=jax.ShapeDtypeStruct((B, S, D), F32),
        input_output_aliases={0: 0},
        compiler_params=_cparams(("parallel", "parallel")),
        name="merge",
    )(x, mods, mods, y0, y1, bonus, g, o0, o1, P, ycv, P, P, P,
      ln_g.reshape(1, W), ln_b.reshape(1, W), gdn_g.reshape(1, B_HEAD_DIM), dw_b.reshape(1, W),
      c_g.reshape(1, W), c_b.reshape(1, W), wb, wo)


def _ffn_pre_kernel(x_ref, g_ref, shb_ref, scb_ref, shc_ref, scc_ref, wr_ref, rb_ref, h_ref, e_ref, w_ref,
                    *, ctx_len):
    i = pl.program_id(1)
    n = x_ref.shape[1]
    mods = (shb_ref[0], scb_ref[0], shc_ref[0], scc_ref[0])
    h = _modulated(x_ref[0], g_ref[...], mods, i * n, ctx_len)
    h_ref[0] = h
    logits = lax.dot_general(wr_ref[...], h, (((1,), (1,)), ((), ())),
                             preferred_element_type=F32, precision=HIGHEST)
    scores = _sigmoid(logits)
    sel = scores + rb_ref[...]
    rows = [sel[e:e + 1, :] for e in range(N_EXPERTS)]
    srow = [scores[e:e + 1, :] for e in range(N_EXPERTS)]
    gscore = []
    for gi in range(N_GROUPS):
        m = rows[gi * EXP_PER_GROUP:(gi + 1) * EXP_PER_GROUP]
        best = None
        for a in range(EXP_PER_GROUP):
            for b in range(a + 1, EXP_PER_GROUP):
                t = m[a] + m[b]
                best = t if best is None else jnp.maximum(best, t)
        gscore.append(best)
    grp = jnp.zeros_like(gscore[0], dtype=I32)
    gbest = gscore[0]
    for gi in range(1, N_GROUPS):
        better = gscore[gi] > gbest
        grp = jnp.where(better, gi, grp)
        gbest = jnp.where(better, gscore[gi], gbest)
    mem = []
    mem_s = []
    for a in range(EXP_PER_GROUP):
        va = rows[a]
        sa = srow[a]
        for gi in range(1, N_GROUPS):
            va = jnp.where(grp == gi, rows[gi * EXP_PER_GROUP + a], va)
            sa = jnp.where(grp == gi, srow[gi * EXP_PER_GROUP + a], sa)
        mem.append(va)
        mem_s.append(sa)
    neg = jnp.full_like(mem[0], -jnp.inf)
    i1 = jnp.zeros_like(grp)
    b1 = mem[0]
    for a in range(1, EXP_PER_GROUP):
        better = mem[a] > b1
        i1 = jnp.where(better, a, i1)
        b1 = jnp.where(better, mem[a], b1)
    i2 = jnp.zeros_like(grp)
    b2 = neg
    first = True
    for a in range(EXP_PER_GROUP):
        cand = jnp.where(i1 == a, neg, mem[a])
        if first:
            b2 = cand
            first = False
        else:
            better = cand > b2
            i2 = jnp.where(better, a, i2)
            b2 = jnp.where(better, cand, b2)
    s1 = mem_s[0]
    s2 = mem_s[0]
    for a in range(1, EXP_PER_GROUP):
        s1 = jnp.where(i1 == a, mem_s[a], s1)
        s2 = jnp.where(i2 == a, mem_s[a], s2)
    tot = s1 + s2
    e_ref[0, 0:1, :] = grp * EXP_PER_GROUP + i1
    e_ref[0, 1:2, :] = grp * EXP_PER_GROUP + i2
    w_ref[0, 0:1, :] = s1 / tot
    w_ref[0, 1:2, :] = s2 / tot


def _ffn_pre(x, gain, mods, ctx_row, w_router, router_bias, ctx_len):
    B, S, D = x.shape
    nt = S // ROW_TILE
    tokd = pl.BlockSpec((1, ROW_TILE, D), lambda b, i: (b, i, 0))
    full = lambda shape: pl.BlockSpec(shape, lambda b, i: tuple(0 for _ in shape))
    sel = pl.BlockSpec((1, 2, ROW_TILE), lambda b, i: (b * nt + i, 0, 0))
    return pl.pallas_call(
        functools.partial(_ffn_pre_kernel, ctx_len=ctx_len),
        grid=(B, nt),
        in_specs=[tokd, full((1, D))] + _mod_specs(2, ctx_row, (3, 4), D)
                 + [full((N_EXPERTS, D)), full((N_EXPERTS, 1))],
        out_specs=[tokd, sel, sel],
        out_shape=[jax.ShapeDtypeStruct((B, S, D), F32),
                   jax.ShapeDtypeStruct((B * nt, 2, ROW_TILE), I32),
                   jax.ShapeDtypeStruct((B * nt, 2, ROW_TILE), F32)],
        compiler_params=_cparams(("parallel", "parallel")),
        name="ffn_pre",
    )(x, gain.reshape(1, D), mods, mods, mods, mods, w_router.T, router_bias.reshape(N_EXPERTS, 1))


def _moe_dispatch(e, w, n_tok):
    R = MOE_ROWS
    A = 2 * n_tok
    ef = e.reshape(A)
    wf = w.reshape(A)
    onehot = (ef[:, None] == jnp.arange(N_EXPERTS, dtype=I32)[None, :]).astype(I32)
    csum = jnp.cumsum(onehot, axis=0)
    rank = jnp.take_along_axis(csum, ef[:, None], axis=1)[:, 0] - 1
    sizes = csum[-1]
    padded = (sizes + R - 1) // R * R
    pad_ends = jnp.cumsum(padded)
    pad_starts = pad_ends - padded
    dest = pad_starts[ef] + rank
    n_blocks = -(-A // R) + N_EXPERTS
    rows = n_blocks * R
    pair = jnp.arange(A, dtype=I32)
    ridx = jnp.arange(rows, dtype=I32)
    table = jnp.stack([A + ridx % R, jnp.zeros((rows,), I32)], axis=1)
    table = table.at[dest].set(jnp.stack([pair, lax.bitcast_convert_type(wf, I32)], axis=1))
    row_dst = table[:, 0]
    row_w = lax.bitcast_convert_type(table[:, 1], F32)
    row_src = jnp.where(row_dst >= A, 0, jnp.where(row_dst >= n_tok, row_dst - n_tok, row_dst))
    block_e = jnp.minimum(jnp.searchsorted(pad_ends, jnp.arange(n_blocks, dtype=I32) * R, side='right'),
                          N_EXPERTS - 1).astype(I32)
    return row_src.reshape(n_blocks, 1, R), row_dst.reshape(n_blocks, 1, R), row_w.reshape(rows, 1), block_e


def _moe_kernel(be_ref, src_ref, nsrc_ref, dst_ref, rw_ref, h_hbm, wg_ref, wu_ref, wd_ref, y_hbm,
                xbuf, ybuf, sem_in, sem_out):
    del be_ref
    R = MOE_ROWS
    b = pl.program_id(0)
    nb = pl.num_programs(0)
    slot = lax.rem(b, 2)
    other = 1 - slot

    def row_in(idx_ref, r, sl):
        return pltpu.make_async_copy(h_hbm.at[pl.ds(idx_ref[0, 0, r], 1)], xbuf.at[sl, pl.ds(r, 1)],
                                     sem_in.at[sl])

    def row_out(r, sl):
        return pltpu.make_async_copy(ybuf.at[sl, pl.ds(r, 1)], y_hbm.at[pl.ds(dst_ref[0, 0, r], 1)],
                                     sem_out.at[sl])

    @pl.when(b == 0)
    def _():
        for r in range(R):
            row_in(src_ref, r, slot).start()

    for r in range(R):
        row_in(src_ref, r, slot).wait()
    for r in range(R):
        row_in(nsrc_ref, r, other).start()
    xb = xbuf[slot].astype(BF16)
    gate = _dot(xb, wg_ref[0])
    up = _dot(xb, wu_ref[0])
    act = (_silu(gate) * up).astype(BF16)
    ybuf[slot] = _dot(act, wd_ref[0]) * rw_ref[...]

    @pl.when(b >= 1)
    def _():
        for r in range(R):
            row_out(r, other).wait()

    for r in range(R):
        row_out(r, slot).start()

    @pl.when(b == nb - 1)
    def _():
        for r in range(R):
            row_out(r, slot).wait()
        for r in range(R):
            row_in(nsrc_ref, r, other).wait()


def _moe(h2, row_src, row_dst, row_w, block_e, wg, wu, wd):
    n_tok, D = h2.shape
    n_blocks = block_e.shape[0]
    R = MOE_ROWS
    F = wg.shape[-1]
    grid_spec = pltpu.PrefetchScalarGridSpec(
        num_scalar_prefetch=1,
        grid=(n_blocks,),
        in_specs=[pl.BlockSpec((1, 1, R), lambda b, be: (b, 0, 0), memory_space=pltpu.SMEM),
                  pl.BlockSpec((1, 1, R), lambda b, be: (jnp.minimum(b + 1, n_blocks - 1), 0, 0),
                               memory_space=pltpu.SMEM),
                  pl.BlockSpec((1, 1, R), lambda b, be: (b, 0, 0), memory_space=pltpu.SMEM),
                  pl.BlockSpec((R, 1), lambda b, be: (b, 0)),
                  pl.BlockSpec(memory_space=pl.ANY),
                  pl.BlockSpec((1, D, F), lambda b, be: (be[b], 0, 0)),
                  pl.BlockSpec((1, D, F), lambda b, be: (be[b], 0, 0)),
                  pl.BlockSpec((1, F, D), lambda b, be: (be[b], 0, 0))],
        out_specs=pl.BlockSpec(memory_space=pl.ANY),
        scratch_shapes=[pltpu.VMEM((2, R, D), F32), pltpu.VMEM((2, R, D), F32),
                        pltpu.SemaphoreType.DMA((2,)), pltpu.SemaphoreType.DMA((2,))],
    )
    return pl.pallas_call(
        _moe_kernel,
        grid_spec=grid_spec,
        out_shape=jax.ShapeDtypeStruct((2 * n_tok + R, D), F32),
        compiler_params=_cparams(("arbitrary",)),
        name="moe",
    )(block_e, row_src, row_src, row_dst, row_w, h2, wg, wu, wd)


def _ffn_add_kernel(x_ref, m5b_ref, m5c_ref, y0_ref, y1_ref, o_ref, *, ctx_len):
    i = pl.program_id(1)
    n = x_ref.shape[1]
    row = i * n + lax.broadcasted_iota(I32, (n, 1), 0)
    m5 = jnp.where(row < ctx_len, m5c_ref[0], m5b_ref[0])
    o_ref[0] = x_ref[0] + m5 * (y0_ref[...] + y1_ref[...])


def _ffn_add(x, mods, ctx_row, y, ctx_len):
    B, S, D = x.shape
    nt = S // ROW_TILE
    tokd = pl.BlockSpec((1, ROW_TILE, D), lambda b, i: (b, i, 0))
    yspec = lambda slot: pl.BlockSpec((ROW_TILE, D), lambda b, i: (slot * B * nt + b * nt + i, 0))
    return pl.pallas_call(
        functools.partial(_ffn_add_kernel, ctx_len=ctx_len),
        grid=(B, nt),
        in_specs=[tokd] + _mod_specs(2, ctx_row, (5,), D) + [yspec(0), yspec(1)],
        out_specs=tokd,
        out_shape=jax.ShapeDtypeStruct((B, S, D), F32),
        input_output_aliases={0: 0},
        compiler_params=_cparams(("parallel", "parallel")),
        name="ffn_add",
    )(x, mods, mods, y, y)


def _final_kernel(x_ref, m5_ref, y0_ref, y1_ref, g_ref, o_ref):
    x = x_ref[0] + m5_ref[0] * (y0_ref[...] + y1_ref[...])
    o_ref[0] = x * lax.rsqrt(jnp.mean(x * x, axis=-1, keepdims=True) + EPS) * g_ref[...]


def _final(x, mods, y, gain, ctx_len):
    B, S, D = x.shape
    nt = S // ROW_TILE
    skip = ctx_len // ROW_TILE
    tok_in = pl.BlockSpec((1, ROW_TILE, D), lambda b, i: (b, i + skip, 0))
    yspec = lambda slot: pl.BlockSpec((ROW_TILE, D), lambda b, i: (slot * B * nt + b * nt + i + skip, 0))
    return pl.pallas_call(
        _final_kernel,
        grid=(B, nt - skip),
        in_specs=[tok_in, pl.BlockSpec((1, 1, D), lambda b, i: (b, 0, 5)), yspec(0), yspec(1),
                  pl.BlockSpec((1, D), lambda b, i: (0, 0))],
        out_specs=pl.BlockSpec((1, ROW_TILE, D), lambda b, i: (b, i, 0)),
        out_shape=jax.ShapeDtypeStruct((B, S - ctx_len, D), F32),
        compiler_params=_cparams(("parallel", "parallel")),
        name="final_norm",
    )(x, mods, y, y, gain.reshape(1, D))


def _pad_w_in(w):
    W = BRANCH_WIDTH
    a = w[:, :A_COLS]
    qkvz = w[:, A_COLS:A_COLS + 4 * W]
    ab = w[:, A_COLS + 4 * W:A_COLS + 4 * W + 4 * B_HEADS]
    rest = w[:, A_COLS + 4 * W + 4 * B_HEADS:]
    ab = jnp.pad(ab, ((0, 0), (0, LANES - 4 * B_HEADS)))
    out = jnp.concatenate([a, ab, qkvz, rest], axis=1).astype(BF16)
    assert out.shape[1] == N_IN_PAD
    return out


def kernel(x, c, ctx, c_ctx, w_ada, b_ada, norm_mix, norm_ffn, norm_final, w_in, rwkv_mu, rwkv_w0, rwkv_w2, rwkv_a0, rwkv_a2, rwkv_g2, rwkv_kk, rwkv_ka, rwkv_rk, rwkv_ln_g, rwkv_ln_b, gdn_conv, gdn_A_log, gdn_dt_bias, gdn_norm, conf_dw, conf_dw_b, conf_ln_g, conf_ln_b, w_branch, w_out, w_router, router_bias, w_e_gate, w_e_up, w_e_down):
    B, T, D = x.shape
    ctx_len = ctx.shape[1]
    depth = w_in.shape[0]
    assert ctx_len == ROW_TILE and T % ROW_TILE == 0 and T % GRID_W == 0
    S = ctx_len + T
    n_tok = B * S

    xs = jnp.concatenate([ctx, x], axis=1)
    rows = -(-(B + 1) // SUBLANES) * SUBLANES
    svec = jnp.concatenate([c, c_ctx[None, :], jnp.zeros((rows - B - 1, D), F32)], axis=0)
    mods_all = _ada(svec, w_ada, b_ada)
    ctx_row = B

    out = None
    for l in range(depth):
        mods = mods_all[l].reshape(rows, 1, 6 * D)
        P = _inproj(xs, norm_mix[l], mods, ctx_row, _pad_w_in(w_in[l]), ctx_len)
        prep = _rwkv_prep(P, rwkv_mu[l], rwkv_w0[l], rwkv_w2[l], rwkv_a0[l], rwkv_a2[l], rwkv_g2[l],
                          rwkv_kk[l], rwkv_ka[l], rwkv_rk[l])
        bonus, gate = prep[11], prep[12]
        y0, y1 = _rwkv_scan(prep[:11], ctx_len)
        q, k, v, gcol, grow = _gdn_prep(P, gdn_conv[l], gdn_A_log[l], gdn_dt_bias[l])
        o0, o1 = _gdn_scan(q, k, v, gcol, grow, ctx_len)
        ycv = _conformer_conv(P, conf_dw[l], ctx_len)
        xs = _merge(xs, mods, ctx_row, y0, y1, bonus, gate, o0, o1, P, ycv,
                    rwkv_ln_g[l], rwkv_ln_b[l], gdn_norm[l], conf_dw_b[l], conf_ln_g[l], conf_ln_b[l],
                    w_branch[l].astype(BF16), w_out[l].astype(BF16), ctx_len)
        h2, e, w = _ffn_pre(xs, norm_ffn[l], mods, ctx_row, w_router, router_bias, ctx_len)
        e = e.transpose(1, 0, 2).reshape(2, n_tok)
        w = w.transpose(1, 0, 2).reshape(2, n_tok)
        row_src, row_dst, row_w, block_e = _moe_dispatch(e, w, n_tok)
        y = _moe(h2.reshape(n_tok, D), row_src, row_dst, row_w, block_e,
                 w_e_gate[l].astype(BF16), w_e_up[l].astype(BF16), w_e_down[l].astype(BF16))
        if l < depth - 1:
            xs = _ffn_add(xs, mods, ctx_row, y, ctx_len)
        else:
            out = _final(xs, mods, y, norm_final, ctx_len)
    return out
```

```python
import functools
import math

import jax
import jax.numpy as jnp
from jax import lax
from jax.experimental import pallas as pl
from jax.experimental.pallas import tpu as pltpu

F32 = jnp.float32
BF16 = jnp.bfloat16
I32 = jnp.int32
HIGHEST = lax.Precision.HIGHEST

EPS = 1e-6
LN_X_EPS = 64e-5
GRID_W = 64
BRANCH_WIDTH = 512
A_HEADS = 8
A_HEAD_DIM = 64
B_HEADS = 4
B_HEAD_DIM = 128
DECAY_LORA = 64
SHORT_CONV = 5
DW_CONV = 31
N_EXPERTS = 16
N_GROUPS = 4
EXP_PER_GROUP = 4
D_EXPERT = 512

ROW_TILE = 256
CHUNK = 64
RWKV_SCAN_CHUNKS = 4
MOE_ROWS = 128
LANES = 128
SUBLANES = 8
HALO_ROWS = 16
VMEM_LIMIT = 56 * 1024 * 1024

A_COLS = 1920
COL_A = 0
COL_AB = 1920
COL_Q = 2048
COL_Z = 3584
COL_C = 4096
COL_G = 5120
N_IN_PAD = 8192
LORA_W = 3 * LANES


def _cparams(sem):
    return pltpu.CompilerParams(dimension_semantics=sem, vmem_limit_bytes=VMEM_LIMIT)


def _sigmoid(x):
    return 1.0 / (1.0 + jnp.exp(-x))


def _silu(x):
    return x * _sigmoid(x)


def _softplus(x):
    return jnp.maximum(x, 0.0) + jnp.log(1.0 + jnp.exp(-jnp.abs(x)))


def _dot(a, b):
    return jnp.dot(a, b, preferred_element_type=F32)


def _dot_nt(a, b):
    return lax.dot_general(a, b, (((1,), (1,)), ((), ())), preferred_element_type=F32)


def _bf16_parts(x, n):
    parts = []
    r = x
    for _ in range(n):
        p = r.astype(BF16)
        parts.append(p)
        r = r - p.astype(F32)
    return parts


def _dot_sel_left(m, x, n=3):
    acc = None
    for p in _bf16_parts(x, n):
        t = _dot(m, p)
        acc = t if acc is None else acc + t
    return acc


def _dot_sel_right(x, m, n=2):
    acc = None
    for p in _bf16_parts(x, n):
        t = _dot(p, m)
        acc = t if acc is None else acc + t
    return acc


def _group_ones(n, group):
    r = lax.broadcasted_iota(I32, (n, n), 0) // group
    c = lax.broadcasted_iota(I32, (n, n), 1) // group
    return jnp.where(r == c, 1.0, 0.0).astype(BF16)


def _chunk_tri(n, lower):
    r = lax.broadcasted_iota(I32, (n, n), 0)
    c = lax.broadcasted_iota(I32, (n, n), 1)
    same = (r // CHUNK) == (c // CHUNK)
    keep = (c <= r) if lower else (c >= r)
    return jnp.where(same & keep, 1.0, 0.0).astype(BF16)


def _shift_down(x, prev_rows, s):
    n = x.shape[0]
    hr = prev_rows.shape[0]
    out = pltpu.roll(x, s, axis=0)
    row = lax.broadcasted_iota(I32, (n, 1), 0)
    for q in range(s):
        out = jnp.where(row == q, prev_rows[hr - s + q:hr - s + q + 1, :], out)
    return out


def _shift_up(x, next_rows, s):
    n = x.shape[0]
    out = pltpu.roll(x, n - s, axis=0)
    row = lax.broadcasted_iota(I32, (n, 1), 0)
    for q in range(s):
        out = jnp.where(row == n - s + q, next_rows[q:q + 1, :], out)
    return out


def _halo_rows(p_prev_ref, p_next_ref, i, n_tiles):
    prev_rows = p_prev_ref[0].astype(F32)
    next_rows = p_next_ref[0].astype(F32)
    prev_rows = jnp.where(i <= 1, 0.0, prev_rows)
    next_rows = jnp.where((i == 0) | (i == n_tiles - 1), 0.0, next_rows)
    return prev_rows, next_rows


def _halo_specs(width, col, tiles_per_batch):
    assert col % width == 0
    col_block = col // width
    sub_per_tile = ROW_TILE // HALO_ROWS
    last = tiles_per_batch * sub_per_tile - 1
    main = pl.BlockSpec((1, ROW_TILE, width), lambda b, i: (b, i, col_block))
    prev = pl.BlockSpec((1, HALO_ROWS, width),
                        lambda b, i: (b, jnp.maximum(i * sub_per_tile - 1, 0), col_block))
    nxt = pl.BlockSpec((1, HALO_ROWS, width),
                       lambda b, i: (b, jnp.minimum((i + 1) * sub_per_tile, last), col_block))
    return [main, prev, nxt]


def _ada_kernel(s_ref, w_ref, b_ref, o_ref):
    s = s_ref[...]
    o_ref[0] = jnp.dot(_silu(s), w_ref[0], preferred_element_type=F32, precision=HIGHEST) + b_ref[0]


def _ada(svec, w_ada, b_ada):
    L, D, N = w_ada.shape
    rows = svec.shape[0]
    tn = 1536
    return pl.pallas_call(
        _ada_kernel,
        grid=(L, N // tn),
        in_specs=[pl.BlockSpec((rows, D), lambda l, j: (0, 0)),
                  pl.BlockSpec((1, D, tn), lambda l, j: (l, 0, j)),
                  pl.BlockSpec((1, 1, tn), lambda l, j: (l, 0, j))],
        out_specs=pl.BlockSpec((1, rows, tn), lambda l, j: (l, 0, j)),
        out_shape=jax.ShapeDtypeStruct((L, rows, N), F32),
        compiler_params=_cparams(("parallel", "parallel")),
        name="ada",
    )(svec, w_ada, b_ada.reshape(L, 1, N))


def _store_token_tiles(ref, x):
    w = x.shape[1] // SUBLANES
    for s in range(SUBLANES):
        ref[:, s, :] = x[:, s * w:(s + 1) * w]


def _load_token_tiles(ref):
    return jnp.concatenate([ref[:, s, :] for s in range(SUBLANES)], axis=1)


def _modulated(x, gain, mods, row0, ctx_len):
    sh_b, sc_b, sh_c, sc_c = mods
    y = x * lax.rsqrt(jnp.mean(x * x, axis=-1, keepdims=True) + EPS) * gain
    row = row0 + lax.broadcasted_iota(I32, (x.shape[0], 1), 0)
    isc = row < ctx_len
    shift = jnp.where(isc, sh_c, sh_b)
    scale = jnp.where(isc, sc_c, sc_b)
    return y * (1.0 + scale) + shift


def _mod_specs(n_grid, ctx_row, chunks, d):
    def spec(row_of, c):
        if n_grid == 2:
            return pl.BlockSpec((1, 1, d), lambda b, i: (row_of(b), 0, c))
        return pl.BlockSpec((1, 1, d), lambda b, i, j: (row_of(b), 0, c))
    return ([spec(lambda b: b, c) for c in chunks] + [spec(lambda b: ctx_row, c) for c in chunks])


def _inproj_kernel(x_ref, g_ref, shb_ref, scb_ref, shc_ref, scc_ref, w_ref, o_ref, h_scr, *, tm, ctx_len):
    i = pl.program_id(1)
    j = pl.program_id(2)

    @pl.when(j == 0)
    def _():
        mods = (shb_ref[0], scb_ref[0], shc_ref[0], scc_ref[0])
        h_scr[...] = _modulated(x_ref[0], g_ref[...], mods, i * tm, ctx_len).astype(BF16)

    o_ref[0] = _dot(h_scr[...], w_ref[...]).astype(o_ref.dtype)


def _inproj(x, gain, mods, ctx_row, wp, ctx_len):
    B, S, D = x.shape
    N = wp.shape[1]
    tm = S // 2
    tn = 1024
    return pl.pallas_call(
        functools.partial(_inproj_kernel, tm=tm, ctx_len=ctx_len),
        grid=(B, S // tm, N // tn),
        in_specs=[pl.BlockSpec((1, tm, D), lambda b, i, j: (b, i, 0)),
                  pl.BlockSpec((1, D), lambda b, i, j: (0, 0))]
                 + _mod_specs(3, ctx_row, (0, 1), D)
                 + [pl.BlockSpec((D, tn), lambda b, i, j: (0, j))],
        out_specs=pl.BlockSpec((1, tm, tn), lambda b, i, j: (b, i, j)),
        out_shape=jax.ShapeDtypeStruct((B, S, N), BF16),
        scratch_shapes=[pltpu.VMEM((tm, D), BF16)],
        compiler_params=_cparams(("parallel", "parallel", "arbitrary")),
        name="inproj",
    )(x, gain.reshape(1, D), mods, mods, mods, mods, wp)


def _rwkv_prep_kernel(r_ref, rp_ref, rn_ref, k_ref, kp_ref_, kn_ref, vv_ref, vp_ref, vn_ref,
                      x_ref, xp_ref, xn_ref,
                      mu_ref, w0_ref, w2_ref, a0_ref, a2_ref, g2_ref, kk_ref, ka_ref, rk_ref,
                      rt0_ref, kp0_ref, bt0_ref, kt0_ref, pc0_ref,
                      rt1_ref, kp1_ref, bt1_ref, kt1_ref, pc1_ref,
                      v_ref, bonus_ref, g_ref, *, n_tiles):
    i = pl.program_id(1)
    W = BRANCH_WIDTH

    def shifted(main, prv, nxt, col, width):
        p = main[0].astype(F32)
        prev_rows, next_rows = _halo_rows(prv, nxt, i, n_tiles)
        mu = mu_ref[:, col:col + width]
        return (p + mu[0:1] * (_shift_down(p, prev_rows, 1) - p)
                + mu[1:2] * (_shift_up(p, next_rows, 1) - p))

    r = shifted(r_ref, rp_ref, rn_ref, 0, W)
    k = shifted(k_ref, kp_ref_, kn_ref, W, W)
    v = shifted(vv_ref, vp_ref, vn_ref, 2 * W, W)
    lora = shifted(x_ref, xp_ref, xn_ref, 3 * W, LORA_W)
    xw = jnp.tanh(lora[:, 0:LANES])
    xa = lora[:, LANES:2 * LANES]
    xg = _sigmoid(lora[:, 2 * LANES:3 * LANES])
    lane_hi = lax.broadcasted_iota(I32, xw.shape, 1) >= DECAY_LORA

    g_ref[0] = _dot(xg.astype(BF16), g2_ref[...].astype(BF16)).astype(g_ref.dtype)
    v_ref[0] = v.astype(BF16)

    gsum = _group_ones(W, A_HEAD_DIM)
    kx = k * kk_ref[...]
    kk = kx * lax.rsqrt(_dot_sel_right(kx * kx, gsum) + EPS)
    bonus_ref[0] = (_dot_sel_right(r * k * rk_ref[...], gsum) * v).astype(bonus_ref.dtype)

    w2 = w2_ref[...].astype(BF16)
    a2 = a2_ref[...].astype(BF16)
    n = r.shape[0]
    outs = ((rt0_ref, kp0_ref, bt0_ref, kt0_ref, pc0_ref), (rt1_ref, kp1_ref, bt1_ref, kt1_ref, pc1_ref))
    for d in range(2):
        keep = lane_hi if d == 1 else jnp.logical_not(lane_hi)
        lw = w0_ref[d:d + 1, :] + _dot(jnp.where(keep, xw, 0.0).astype(BF16), w2)
        ew = _sigmoid(lw) * math.exp(-0.5)
        a = _sigmoid(a0_ref[d:d + 1, :] + _dot(jnp.where(keep, xa, 0.0).astype(BF16), a2))
        keff = k * (1.0 + (a - 1.0) * ka_ref[...])
        cs = _dot_sel_left(_chunk_tri(n, lower=(d == 0)), ew)
        p_incl = jnp.exp(-cs)
        p_excl = jnp.exp(ew - cs)
        p_inv = jnp.exp(cs)
        rt_ref, kp_ref, bt_ref, kt_ref, pc_ref = outs[d]
        rt_ref[0] = (r * p_incl).astype(BF16)
        kp_ref[0] = (kk * p_excl).astype(BF16)
        bt_ref[0] = (kk * a * p_inv).astype(BF16)
        kt_ref[0] = (keff * p_inv).astype(BF16)
        for c in range(n // CHUNK):
            row = c * CHUNK + (CHUNK - 1 if d == 0 else 0)
            pc_ref[0, c] = p_incl[row:row + 1, :]


def _rwkv_prep(P, mu, w0, w2, a0, a2, g2, k_k, k_a, r_k):
    B, S, _ = P.shape
    W = BRANCH_WIDTH
    nt = S // ROW_TILE
    cpt = ROW_TILE // CHUNK
    halo = (_halo_specs(W, COL_A, nt) + _halo_specs(W, COL_A + W, nt) + _halo_specs(W, COL_A + 2 * W, nt)
            + _halo_specs(LORA_W, COL_A + 3 * W, nt))
    full = lambda shape: pl.BlockSpec(shape, lambda b, i: tuple(0 for _ in shape))
    tok = pl.BlockSpec((1, ROW_TILE, W), lambda b, i: (b, i, 0))
    pcs = pl.BlockSpec((1, cpt, 1, W), lambda b, i: (b, i, 0, 0))
    bf = jax.ShapeDtypeStruct((B, S, W), BF16)
    f32 = jax.ShapeDtypeStruct((B, S, W), F32)
    pcshape = jax.ShapeDtypeStruct((B, S // CHUNK, 1, W), F32)
    return pl.pallas_call(
        functools.partial(_rwkv_prep_kernel, n_tiles=nt),
        grid=(B, nt),
        in_specs=halo + [full((2, A_COLS)), full((2, W)), full((2 * DECAY_LORA, W)), full((2, W)),
                         full((2 * DECAY_LORA, W)), full((LANES, W)), full((1, W)), full((1, W)),
                         full((1, W))],
        out_specs=[tok, tok, tok, tok, pcs, tok, tok, tok, tok, pcs, tok, tok, tok],
        out_shape=[bf, bf, bf, bf, pcshape, bf, bf, bf, bf, pcshape, bf, bf, bf],
        compiler_params=_cparams(("parallel", "parallel")),
        name="rwkv_prep",
    )(*([P] * 12), mu, w0, w2.reshape(2 * DECAY_LORA, W), a0, a2.reshape(2 * DECAY_LORA, W), g2,
      k_k.reshape(1, W), k_a.reshape(1, W), r_k.reshape(1, W))


def _unit_lower_inverse(a_neg, steps):
    n = a_neg.shape[0]
    eye = jnp.where(lax.broadcasted_iota(I32, (n, n), 0) == lax.broadcasted_iota(I32, (n, n), 1), 1.0, 0.0)
    t = eye + a_neg
    xp = a_neg.astype(BF16)
    for _ in range(steps - 1):
        x2 = _dot(xp, xp)
        xp = x2.astype(BF16)
        t = t + _dot(t.astype(BF16), xp)
    return t


def _run_chains(chains):
    while chains:
        alive = []
        for g in chains:
            try:
                next(g)
                alive.append(g)
            except StopIteration:
                pass
        chains = alive


def _stack_heads(x, lane_hi):
    z = jnp.zeros_like(x)
    return jnp.concatenate([jnp.where(lane_hi, z, x), jnp.where(lane_hi, x, z)], axis=0)


def _rwkv_scan_kernel(rt0, kp0, bt0, kt0, v0, pc0, rt1, kp1, bt1, kt1, v1, pc1, y0_ref, y1_ref, st_ref):
    s = pl.program_id(1)

    @pl.when(s == 0)
    def _():
        st_ref[...] = jnp.zeros_like(st_ref)

    C = CHUNK
    n2 = 2 * C
    ri = lax.broadcasted_iota(I32, (n2, n2), 0)
    ci = lax.broadcasted_iota(I32, (n2, n2), 1)
    same = (ri // C) == (ci // C)
    tr = ri % C
    ts = ci % C
    lane_hi = lax.broadcasted_iota(I32, (C, n2), 1) >= C
    eye = jnp.where(ri == ci, 1.0, 0.0)
    dirs = ((rt0, kp0, bt0, kt0, v0, pc0, y0_ref), (rt1, kp1, bt1, kt1, v1, pc1, y1_ref))
    nch = rt0.shape[1] // C
    ready = {}

    def intra(d, j, cc):
        rt, kp, bt, kt, vv, pc, _ = dirs[d]
        strict = same & ((ts < tr) if d == 0 else (ts > tr))
        incl = same & ((ts <= tr) if d == 0 else (ts >= tr))
        sl = slice(n2 * j, n2 * (j + 1))
        rows = slice(cc * C, (cc + 1) * C)
        kap = kp[0, rows, sl]
        r_ = rt[0, rows, sl]
        b_ = bt[0, rows, sl]
        k_ = kt[0, rows, sl]
        v_st = _stack_heads(vv[0, rows, sl], lane_hi)
        aa = _dot_nt(jnp.concatenate([_stack_heads(kap, lane_hi), _stack_heads(r_, lane_hi)], axis=0),
                     jnp.concatenate([b_, b_, k_, k_], axis=0))
        yield
        x = jnp.where(strict, -aa[:n2, :n2], 0.0)
        a_ak = jnp.where(strict, aa[:n2, n2:], 0.0).astype(BF16)
        a_rb = jnp.where(incl, aa[n2:, :n2], 0.0).astype(BF16)
        a_rk = jnp.where(incl, aa[n2:, n2:], 0.0).astype(BF16)
        t = eye + x
        xp = x.astype(BF16)
        b_t = _stack_heads(b_, lane_hi).astype(F32).T.astype(BF16)
        k_t = _stack_heads(k_, lane_hi).astype(F32).T.astype(BF16)
        sv = _dot(jnp.concatenate([a_ak, a_rk, k_t], axis=0), v_st)
        akv = sv[:n2]
        arkv = sv[n2:2 * n2]
        kv0 = sv[2 * n2:]
        x2 = _dot(xp, xp)
        yield
        for _ in range(4):
            xp = x2.astype(BF16)
            tx = _dot(jnp.concatenate([t.astype(BF16), xp], axis=0), xp)
            yield
            t = t + tx[:n2]
            x2 = tx[n2:]
        t = t + _dot(t.astype(BF16), x2.astype(BF16))
        yield
        pcm = jnp.broadcast_to(pc[0, cc][:, sl], (n2, n2)).T
        ready[(d, j, cc)] = (jnp.concatenate([kap, r_], axis=0), t.astype(BF16), akv, arkv, kv0,
                             jnp.concatenate([a_rb, b_t], axis=0), pcm)

    def state(d, j):
        y_ref = dirs[d][6]
        sl = slice(n2 * j, n2 * (j + 1))
        h = st_ref[d, j]
        for cc in (range(nch) if d == 0 else range(nch - 1, -1, -1)):
            while (d, j, cc) not in ready:
                yield
            kap_r, t_b, akv, arkv, kv0, arb_bt, pcm = ready.pop((d, j, cc))
            krh = _dot(kap_r, h.astype(BF16))
            yield
            rhs = akv + _stack_heads(krh[:C], lane_hi)
            u_st = -_dot(t_b, rhs.astype(BF16))
            yield
            yd = _dot(arb_bt, u_st.astype(BF16))
            yield
            y_st = yd[:n2] + arkv
            y_ref[0, cc * C:(cc + 1) * C, sl] = (y_st[:C] + y_st[C:] + krh[C:]).astype(y_ref.dtype)
            h = pcm * (h + kv0 + yd[n2:])
        st_ref[d, j] = h

    units = [(d, j) for d in range(2) for j in range(BRANCH_WIDTH // n2)]
    chains = []
    for pos in range(nch):
        for d, j in units:
            chains.append(intra(d, j, pos if d == 0 else nch - 1 - pos))
    _run_chains(chains + [state(d, j) for d, j in units])


def _scan_tile_maps(nt_ctx, nt):
    def t0(s):
        return s

    def t1(s):
        return jnp.where(s < nt_ctx, nt_ctx - 1 - s, nt + nt_ctx - 1 - s)

    return t0, t1


def _rwkv_scan(prep, ctx_len):
    rt0, kp0, bt0, kt0, pc0, rt1, kp1, bt1, kt1, pc1, v = prep
    B, S, W = v.shape
    rows = RWKV_SCAN_CHUNKS * CHUNK
    c0, c1 = _scan_tile_maps(ctx_len // rows, S // rows)
    tok = lambda cm: pl.BlockSpec((1, rows, W), lambda b, s: (b, cm(s), 0))
    pcs = lambda cm: pl.BlockSpec((1, RWKV_SCAN_CHUNKS, 1, W), lambda b, s: (b, cm(s), 0, 0))
    y = jax.ShapeDtypeStruct((B, S, W), BF16)
    return pl.pallas_call(
        _rwkv_scan_kernel,
        grid=(B, S // rows),
        in_specs=[tok(c0), tok(c0), tok(c0), tok(c0), tok(c0), pcs(c0),
                  tok(c1), tok(c1), tok(c1), tok(c1), tok(c1), pcs(c1)],
        out_specs=[tok(c0), tok(c1)],
        out_shape=[y, y],
        scratch_shapes=[pltpu.VMEM((2, W // (2 * CHUNK), 2 * CHUNK, 2 * CHUNK), F32)],
        compiler_params=_cparams(("parallel", "arbitrary")),
        name="rwkv_scan",
    )(rt0, kp0, bt0, kt0, v, pc0, rt1, kp1, bt1, kt1, v, pc1)


def _gdn_prep_kernel(qi_ref, qp_ref, qn_ref, ki_ref, kp_ref, kn_ref, vi_ref, vp_ref, vn_ref,
                     ab_ref, cw_ref, al_ref, dt_ref,
                     q_ref, k_ref, v_ref, gcol_ref, grow_ref, *, n_tiles):
    i = pl.program_id(1)
    W = BRANCH_WIDTH
    half = SHORT_CONV // 2

    def conv_silu(main, prv, nxt, col):
        x = main[0].astype(F32)
        prev_rows, next_rows = _halo_rows(prv, nxt, i, n_tiles)
        cw = cw_ref[:, col:col + W]
        acc = cw[half:half + 1] * x
        for s in range(1, half + 1):
            acc = acc + cw[half - s:half - s + 1] * _shift_down(x, prev_rows, s)
            acc = acc + cw[half + s:half + s + 1] * _shift_up(x, next_rows, s)
        return _silu(acc)

    yq = conv_silu(qi_ref, qp_ref, qn_ref, 0)
    yk = conv_silu(ki_ref, kp_ref, kn_ref, W)
    v_ref[0] = conv_silu(vi_ref, vp_ref, vn_ref, 2 * W).astype(v_ref.dtype)
    for hd in range(B_HEADS):
        sl = slice(hd * B_HEAD_DIM, (hd + 1) * B_HEAD_DIM)
        q = yq[:, sl]
        q_ref[0, :, sl] = (q * lax.rsqrt(jnp.sum(q * q, axis=-1, keepdims=True) + EPS)
                           * (B_HEAD_DIM ** -0.5)).astype(q_ref.dtype)
        k = yk[:, sl]
        k_ref[0, :, sl] = (k * lax.rsqrt(jnp.sum(k * k, axis=-1, keepdims=True) + EPS)).astype(k_ref.dtype)

    ab = ab_ref[0].astype(F32)
    n = ab.shape[0]
    lane = lax.broadcasted_iota(I32, ab.shape, 1)
    g = -jnp.exp(al_ref[...]) * _softplus(ab + dt_ref[...])
    g = jnp.where(lane < 2 * B_HEADS, g, 0.0)
    gc_f = _dot_sel_left(_chunk_tri(n, lower=True), g)
    gc_b = _dot_sel_left(_chunk_tri(n, lower=False), g)
    gc = jnp.where(lane < B_HEADS, gc_f, gc_b)
    beta = _sigmoid(ab)
    out = jnp.where(lane < 2 * B_HEADS, gc, jnp.where(lane < 4 * B_HEADS, beta, 0.0))
    gcol_ref[0] = out
    grow_ref[0] = out.T[:4 * B_HEADS, :]


def _gdn_prep(P, conv_w, A_log, dt_bias):
    B, S, _ = P.shape
    W = BRANCH_WIDTH
    nt = S // ROW_TILE
    halo = _halo_specs(W, COL_Q, nt) + _halo_specs(W, COL_Q + W, nt) + _halo_specs(W, COL_Q + 2 * W, nt)
    pad = jnp.zeros((1, LANES - 2 * B_HEADS), F32)
    al = jnp.concatenate([A_log.reshape(1, 2 * B_HEADS), pad], axis=1)
    dt = jnp.concatenate([dt_bias.reshape(1, 2 * B_HEADS), pad], axis=1)
    full = lambda shape: pl.BlockSpec(shape, lambda b, i: tuple(0 for _ in shape))
    tok = pl.BlockSpec((1, ROW_TILE, W), lambda b, i: (b, i, 0))
    bf16 = jax.ShapeDtypeStruct((B, S, W), BF16)
    return pl.pallas_call(
        functools.partial(_gdn_prep_kernel, n_tiles=nt),
        grid=(B, nt),
        in_specs=halo + [pl.BlockSpec((1, ROW_TILE, LANES), lambda b, i: (b, i, COL_AB // LANES)),
                         full((SHORT_CONV, 3 * W)), full((1, LANES)), full((1, LANES))],
        out_specs=[tok, tok, tok,
                   pl.BlockSpec((1, ROW_TILE, LANES), lambda b, i: (b, i, 0)),
                   pl.BlockSpec((1, 4 * B_HEADS, ROW_TILE), lambda b, i: (b, 0, i))],
        out_shape=[bf16, bf16, bf16, jax.ShapeDtypeStruct((B, S, LANES), F32),
                   jax.ShapeDtypeStruct((B, 4 * B_HEADS, S), F32)],
        compiler_params=_cparams(("parallel", "parallel")),
        name="gdn_prep",
    )(*([P] * 10), conv_w, al, dt)


def _gdn_scan_kernel(q0, k0, v0, gc0, gr0, q1, k1, v1, gc1, gr1, o0_ref, o1_ref, st_ref):
    s = pl.program_id(1)

    @pl.when(s == 0)
    def _():
        st_ref[...] = jnp.zeros_like(st_ref)

    C = CHUNK
    nch = q0.shape[1] // C
    ri = lax.broadcasted_iota(I32, (C, C), 0)
    ci = lax.broadcasted_iota(I32, (C, C), 1)
    dirs = ((q0, k0, v0, gc0, gr0, o0_ref), (q1, k1, v1, gc1, gr1, o1_ref))
    eye = jnp.where(ri == ci, 1.0, 0.0)
    ready = {}

    def intra(d, hd, cc):
        qr, kr, vr, gcr, grr, _ = dirs[d]
        incl = (ci <= ri) if d == 0 else (ci >= ri)
        strict = (ci < ri) if d == 0 else (ci > ri)
        last = C - 1 if d == 0 else 0
        rows = slice(cc * C, (cc + 1) * C)
        gcol = gcr[0, rows, :]
        grow = grr[0, cc]
        sl = slice(hd * B_HEAD_DIM, (hd + 1) * B_HEAD_DIM)
        idx = d * B_HEADS + hd
        q = qr[0, rows, sl].astype(F32)
        k = kr[0, rows, sl].astype(F32)
        v = vr[0, rows, sl].astype(F32)
        gc_c = gcol[:, idx:idx + 1]
        gc_r = grow[idx:idx + 1, :]
        beta = gcol[:, 2 * B_HEADS + idx:2 * B_HEADS + idx + 1]
        gc_last = gc_r[:, last:last + 1]
        diff = gc_c - gc_r
        decay_incl = jnp.where(incl, jnp.exp(jnp.where(incl, diff, 0.0)), 0.0)
        decay_strict = jnp.where(strict, decay_incl, 0.0)
        kb = k * beta
        e_gc = jnp.exp(gc_c)
        qk = _dot_nt(jnp.concatenate([kb, q], axis=0).astype(BF16), k.astype(BF16))
        yield
        x = -(qk[:C] * decay_strict)
        attn = (qk[C:] * decay_incl).astype(BF16)
        rhs = jnp.concatenate([v * beta, kb * e_gc], axis=1).astype(BF16)
        q_dec = (q * e_gc).astype(BF16)
        k_dec_t = (k * jnp.exp(gc_last - gc_c)).T.astype(BF16)
        g_last = jnp.exp(gc_last)
        t = eye + x
        xp = x.astype(BF16)
        x2 = _dot(xp, xp)
        yield
        for _ in range(4):
            xp = x2.astype(BF16)
            tx = _dot(jnp.concatenate([t.astype(BF16), xp], axis=0), xp)
            yield
            t = t + tx[:C]
            x2 = tx[C:]
        t = t + _dot(t.astype(BF16), x2.astype(BF16))
        yield
        sol = _dot(t.astype(BF16), rhs)
        yield
        ready[(d, hd, cc)] = (sol[:, :B_HEAD_DIM], sol[:, B_HEAD_DIM:].astype(BF16), q_dec, attn, k_dec_t,
                              g_last)

    def state(d, hd):
        o_ref = dirs[d][5]
        sl = slice(hd * B_HEAD_DIM, (hd + 1) * B_HEAD_DIM)
        st = st_ref[d, hd]
        for cc in (range(nch) if d == 0 else range(nch - 1, -1, -1)):
            while (d, hd, cc) not in ready:
                yield
            u, w, q_dec, attn, k_dec_t, g_last = ready.pop((d, hd, cc))
            ws_qs = _dot(jnp.concatenate([w, q_dec], axis=0), st.astype(BF16))
            yield
            vn_b = (u - ws_qs[:C]).astype(BF16)
            od = _dot(jnp.concatenate([attn, k_dec_t], axis=0), vn_b)
            yield
            o_ref[0, cc * C:(cc + 1) * C, sl] = (ws_qs[C:] + od[:C]).astype(o_ref.dtype)
            st = st * g_last + od[C:]
        st_ref[d, hd] = st

    units = [(d, hd) for d in range(2) for hd in range(B_HEADS)]
    chains = []
    for pos in range(nch):
        for d, hd in units:
            chains.append(intra(d, hd, pos if d == 0 else nch - 1 - pos))
    _run_chains(chains + [state(d, hd) for d, hd in units])


def _gdn_scan(q, k, v, gcol, grow, ctx_len):
    B, S, W = q.shape
    nc = S // CHUNK
    nt = S // ROW_TILE
    cpt = ROW_TILE // CHUNK
    c0, c1 = _scan_tile_maps(ctx_len // ROW_TILE, nt)
    tok = lambda cm: pl.BlockSpec((1, ROW_TILE, W), lambda b, s: (b, cm(s), 0))
    gcs = lambda cm: pl.BlockSpec((1, ROW_TILE, LANES), lambda b, s: (b, cm(s), 0))
    grs = lambda cm: pl.BlockSpec((1, cpt, 4 * B_HEADS, CHUNK), lambda b, s: (b, cm(s), 0, 0))
    grow_c = grow.reshape(B, 4 * B_HEADS, nc, CHUNK).transpose(0, 2, 1, 3)
    o = jax.ShapeDtypeStruct((B, S, W), BF16)
    return pl.pallas_call(
        _gdn_scan_kernel,
        grid=(B, nt),
        in_specs=[tok(c0), tok(c0), tok(c0), gcs(c0), grs(c0), tok(c1), tok(c1), tok(c1), gcs(c1), grs(c1)],
        out_specs=[tok(c0), tok(c1)],
        out_shape=[o, o],
        scratch_shapes=[pltpu.VMEM((2, B_HEADS, B_HEAD_DIM, B_HEAD_DIM), F32)],
        compiler_params=_cparams(("parallel", "arbitrary")),
        name="gdn_scan",
    )(q, k, v, gcol, grow_c, q, k, v, gcol, grow_c)


def _conf_kernel(val_ref, gate_ref, dw_ref, o_ref, ctx_scr, w_scr, h_scr, *, ctx_len, n_rows):
    cblk = pl.program_id(1)
    K = DW_CONV
    half = K // 2
    pad = 2 * SUBLANES
    stride = GRID_W + 2 * pad
    u = val_ref[0].astype(F32) * _sigmoid(gate_ref[0].astype(F32))
    dw = dw_ref[...]

    ctx_scr[...] = jnp.zeros_like(ctx_scr)
    ctx_scr[pad:pad + ctx_len, :] = u[:ctx_len]
    acc = jnp.zeros((ctx_len, LANES), F32)
    for j in range(K):
        acc = acc + dw[j:j + 1] * ctx_scr[pad - half + j:pad - half + j + ctx_len, :]
    o_ref[0, :ctx_len, :] = acc.astype(o_ref.dtype)

    lat = u[ctx_len:]

    @pl.when(cblk < 2)
    def _():
        w_scr[...] = jnp.zeros_like(w_scr)
        for r in range(n_rows):
            w_scr[r * stride + pad:r * stride + pad + GRID_W, :] = lat[r * GRID_W:(r + 1) * GRID_W]
        for r in range(n_rows):
            a = jnp.zeros((GRID_W, LANES), F32)
            for j in range(K):
                o = r * stride + pad - half + j
                a = a + dw[j:j + 1] * w_scr[o:o + GRID_W, :]
            o_ref[0, ctx_len + r * GRID_W:ctx_len + (r + 1) * GRID_W, :] = a.astype(o_ref.dtype)

    @pl.when(cblk >= 2)
    def _():
        hp = half * GRID_W
        h_scr[...] = jnp.zeros_like(h_scr)
        h_scr[hp:hp + n_rows * GRID_W, :] = lat
        for r in range(n_rows):
            a = jnp.zeros((GRID_W, LANES), F32)
            for j in range(K):
                rr = r + j - half
                if 0 <= rr < n_rows:
                    a = a + dw[j:j + 1] * h_scr[hp + rr * GRID_W:hp + (rr + 1) * GRID_W, :]
            o_ref[0, ctx_len + r * GRID_W:ctx_len + (r + 1) * GRID_W, :] = a.astype(o_ref.dtype)


def _conformer_conv(P, dw, ctx_len):
    B, S, _ = P.shape
    W = BRANCH_WIDTH
    n_rows = (S - ctx_len) // GRID_W
    pad = 2 * SUBLANES
    nblk = W // LANES
    return pl.pallas_call(
        functools.partial(_conf_kernel, ctx_len=ctx_len, n_rows=n_rows),
        grid=(B, nblk),
        in_specs=[pl.BlockSpec((1, S, LANES), lambda b, c: (b, 0, COL_C // LANES + c)),
                  pl.BlockSpec((1, S, LANES), lambda b, c: (b, 0, (COL_C + W) // LANES + c)),
                  pl.BlockSpec((DW_CONV, LANES), lambda b, c: (0, c))],
        out_specs=pl.BlockSpec((1, S, LANES), lambda b, c: (b, 0, c)),
        out_shape=jax.ShapeDtypeStruct((B, S, W), BF16),
        scratch_shapes=[pltpu.VMEM((ctx_len + 2 * pad, LANES), F32),
                        pltpu.VMEM((n_rows * (GRID_W + 2 * pad), LANES), F32),
                        pltpu.VMEM(((n_rows + 2 * (DW_CONV // 2)) * GRID_W, LANES), F32)],
        compiler_params=_cparams(("parallel", "parallel")),
        name="conformer_conv",
    )(P, P, dw)


def _merge_kernel(x_ref, m2b_ref, m2c_ref, y0_ref, y1_ref, bonus_ref, g_ref, o0_ref, o1_ref, z_ref,
                  yc_ref, pg0_ref, pg1_ref, pg2_ref, lng_ref, lnb_ref, gng_ref, dwb_ref, cg_ref, cb_ref,
                  wb_ref, wo_ref, out_ref, *, ctx_len):
    i = pl.program_id(1)
    W = BRANCH_WIDTH
    D = x_ref.shape[-1]
    n = x_ref.shape[1]
    y = y0_ref[0].astype(F32) + y1_ref[0].astype(F32)
    gsum = _group_ones(W, A_HEAD_DIM)
    mu = _dot_sel_right(y, gsum) * (1.0 / A_HEAD_DIM)
    yc = y - mu
    var = _dot_sel_right(yc * yc, gsum) * (1.0 / A_HEAD_DIM)
    ya = (yc * lax.rsqrt(var + LN_X_EPS) * lng_ref[...] + lnb_ref[...] + bonus_ref[0].astype(F32)) * g_ref[0].astype(F32)
    o = o0_ref[0].astype(F32) + o1_ref[0].astype(F32)
    z = z_ref[0].astype(F32)
    parts = []
    for hd in range(B_HEADS):
        oh = o[:, hd * B_HEAD_DIM:(hd + 1) * B_HEAD_DIM]
        parts.append(oh * lax.rsqrt(jnp.mean(oh * oh, axis=-1, keepdims=True) + EPS) * gng_ref[...])
    yb = jnp.concatenate(parts, axis=1) * _silu(z)
    c = yc_ref[0].astype(F32) + dwb_ref[...]
    cm = jnp.mean(c, axis=-1, keepdims=True)
    cc = c - cm
    cv = jnp.mean(cc * cc, axis=-1, keepdims=True)
    ycf = _silu(cc * lax.rsqrt(cv + EPS) * cg_ref[...] + cb_ref[...])

    acc = jnp.zeros((n, D), F32)
    for nb, (br, pg_ref) in enumerate(((ya, pg0_ref), (yb, pg1_ref), (ycf, pg2_ref))):
        up = _dot(br.astype(BF16), wb_ref[nb])
        acc = acc + _sigmoid(pg_ref[0].astype(F32)) * up
    mix = _dot(acc.astype(BF16), wo_ref[...])
    row = i * n + lax.broadcasted_iota(I32, (n, 1), 0)
    m2 = jnp.where(row < ctx_len, m2c_ref[0], m2b_ref[0])
    out_ref[0] = x_ref[0] + m2 * mix


def _merge(x, mods, ctx_row, y0, y1, bonus, g, o0, o1, P, ycv, ln_g, ln_b, gdn_g, dw_b, c_g, c_b, wb, wo,
           ctx_len):
    B, S, D = x.shape
    W = BRANCH_WIDTH
    nt = S // ROW_TILE
    assert COL_Z % W == 0 and COL_G % D == 0
    tokw = pl.BlockSpec((1, ROW_TILE, W), lambda b, i: (b, i, 0))
    tokd = pl.BlockSpec((1, ROW_TILE, D), lambda b, i: (b, i, 0))
    gate = lambda nb: pl.BlockSpec((1, ROW_TILE, D), lambda b, i: (b, i, COL_G // D + nb))
    full = lambda shape: pl.BlockSpec(shape, lambda b, i: tuple(0 for _ in shape))
    return pl.pallas_call(
        functools.partial(_merge_kernel, ctx_len=ctx_len),
        grid=(B, nt),
        in_specs=[tokd] + _mod_specs(2, ctx_row, (2,), D)
                 + [tokw, tokw, tokw, tokw, tokw, tokw,
                    pl.BlockSpec((1, ROW_TILE, W), lambda b, i: (b, i, COL_Z // W)),
                    tokw, gate(0), gate(1), gate(2),
                    full((1, W)), full((1, W)), full((1, B_HEAD_DIM)), full((1, W)), full((1, W)),
                    full((1, W)), full((3, W, D)), full((D, D))],
        out_specs=tokd,
        out_shape=jax.ShapeDtypeStruct((B, S, D), F32),
        input_output_aliases={0: 0},
        compiler_params=_cparams(("parallel", "parallel")),
        name="merge",
    )(x, mods, mods, y0, y1, bonus, g, o0, o1, P, ycv, P, P, P,
      ln_g.reshape(1, W), ln_b.reshape(1, W), gdn_g.reshape(1, B_HEAD_DIM), dw_b.reshape(1, W),
      c_g.reshape(1, W), c_b.reshape(1, W), wb, wo)


def _ffn_pre_kernel(x_ref, g_ref, shb_ref, scb_ref, shc_ref, scc_ref, wr_ref, rb_ref, h_ref, e_ref, w_ref,
                    *, ctx_len):
    i = pl.program_id(1)
    n = x_ref.shape[1]
    mods = (shb_ref[0], scb_ref[0], shc_ref[0], scc_ref[0])
    h = _modulated(x_ref[0], g_ref[...], mods, i * n, ctx_len)
    _store_token_tiles(h_ref, h)
    logits = lax.dot_general(wr_ref[...], h, (((1,), (1,)), ((), ())),
                             preferred_element_type=F32, precision=HIGHEST)
    scores = _sigmoid(logits)
    sel = scores + rb_ref[...]
    rows = [sel[e:e + 1, :] for e in range(N_EXPERTS)]
    srow = [scores[e:e + 1, :] for e in range(N_EXPERTS)]
    gscore = []
    for gi in range(N_GROUPS):
        m = rows[gi * EXP_PER_GROUP:(gi + 1) * EXP_PER_GROUP]
        best = None
        for a in range(EXP_PER_GROUP):
            for b in range(a + 1, EXP_PER_GROUP):
                t = m[a] + m[b]
                best = t if best is None else jnp.maximum(best, t)
        gscore.append(best)
    grp = jnp.zeros_like(gscore[0], dtype=I32)
    gbest = gscore[0]
    for gi in range(1, N_GROUPS):
        better = gscore[gi] > gbest
        grp = jnp.where(better, gi, grp)
        gbest = jnp.where(better, gscore[gi], gbest)
    mem = []
    mem_s = []
    for a in range(EXP_PER_GROUP):
        va = rows[a]
        sa = srow[a]
        for gi in range(1, N_GROUPS):
            va = jnp.where(grp == gi, rows[gi * EXP_PER_GROUP + a], va)
            sa = jnp.where(grp == gi, srow[gi * EXP_PER_GROUP + a], sa)
        mem.append(va)
        mem_s.append(sa)
    neg = jnp.full_like(mem[0], -jnp.inf)
    i1 = jnp.zeros_like(grp)
    b1 = mem[0]
    for a in range(1, EXP_PER_GROUP):
        better = mem[a] > b1
        i1 = jnp.where(better, a, i1)
        b1 = jnp.where(better, mem[a], b1)
    i2 = jnp.zeros_like(grp)
    b2 = neg
    first = True
    for a in range(EXP_PER_GROUP):
        cand = jnp.where(i1 == a, neg, mem[a])
        if first:
            b2 = cand
            first = False
        else:
            better = cand > b2
            i2 = jnp.where(better, a, i2)
            b2 = jnp.where(better, cand, b2)
    s1 = mem_s[0]
    s2 = mem_s[0]
    for a in range(1, EXP_PER_GROUP):
        s1 = jnp.where(i1 == a, mem_s[a], s1)
        s2 = jnp.where(i2 == a, mem_s[a], s2)
    tot = s1 + s2
    e_ref[0, 0:1, :] = grp * EXP_PER_GROUP + i1
    e_ref[0, 1:2, :] = grp * EXP_PER_GROUP + i2
    w_ref[0, 0:1, :] = s1 / tot
    w_ref[0, 1:2, :] = s2 / tot


def _ffn_pre(x, gain, mods, ctx_row, w_router, router_bias, ctx_len):
    B, S, D = x.shape
    nt = S // ROW_TILE
    tokd = pl.BlockSpec((1, ROW_TILE, D), lambda b, i: (b, i, 0))
    full = lambda shape: pl.BlockSpec(shape, lambda b, i: tuple(0 for _ in shape))
    sel = pl.BlockSpec((1, 2, ROW_TILE), lambda b, i: (b * nt + i, 0, 0))
    return pl.pallas_call(
        functools.partial(_ffn_pre_kernel, ctx_len=ctx_len),
        grid=(B, nt),
        in_specs=[tokd, full((1, D))] + _mod_specs(2, ctx_row, (3, 4), D)
                 + [full((N_EXPERTS, D)), full((N_EXPERTS, 1))],
        out_specs=[pl.BlockSpec((ROW_TILE, SUBLANES, D // SUBLANES), lambda b, i: (b * nt + i, 0, 0)), sel, sel],
        out_shape=[jax.ShapeDtypeStruct((B * S, SUBLANES, D // SUBLANES), F32),
                   jax.ShapeDtypeStruct((B * nt, 2, ROW_TILE), I32),
                   jax.ShapeDtypeStruct((B * nt, 2, ROW_TILE), F32)],
        compiler_params=_cparams(("parallel", "parallel")),
        name="ffn_pre",
    )(x, gain.reshape(1, D), mods, mods, mods, mods, w_router.T, router_bias.reshape(N_EXPERTS, 1))


def _moe_dispatch(e, w, n_tok):
    R = MOE_ROWS
    A = 2 * n_tok
    ef = e.reshape(A)
    wf = w.reshape(A)
    onehot = (ef[:, None] == jnp.arange(N_EXPERTS, dtype=I32)[None, :]).astype(I32)
    csum = jnp.cumsum(onehot, axis=0)
    rank = jnp.take_along_axis(csum, ef[:, None], axis=1)[:, 0] - 1
    sizes = csum[-1]
    padded = (sizes + R - 1) // R * R
    pad_ends = jnp.cumsum(padded)
    pad_starts = pad_ends - padded
    dest = pad_starts[ef] + rank
    n_blocks = -(-A // R) + N_EXPERTS
    rows = n_blocks * R
    pair = jnp.arange(A, dtype=I32)
    ridx = jnp.arange(rows, dtype=I32)
    table = jnp.stack([A + ridx % R, jnp.zeros((rows,), I32)], axis=1)
    table = table.at[dest].set(jnp.stack([pair, lax.bitcast_convert_type(wf, I32)], axis=1))
    row_dst = table[:, 0]
    row_w = lax.bitcast_convert_type(table[:, 1], F32)
    row_src = jnp.where(row_dst >= A, 0, jnp.where(row_dst >= n_tok, row_dst - n_tok, row_dst))
    block_e = jnp.minimum(jnp.searchsorted(pad_ends, jnp.arange(n_blocks, dtype=I32) * R, side='right'),
                          N_EXPERTS - 1).astype(I32)
    return row_src.reshape(n_blocks, 1, R), row_dst.reshape(n_blocks, 1, R), row_w.reshape(rows, 1), block_e


def _moe_kernel(be_ref, src_ref, nsrc_ref, dst_ref, rw_ref, h_hbm, wg_ref, wu_ref, wd_ref, y_hbm,
                xbuf, ybuf, sem_in, sem_out):
    del be_ref
    R = MOE_ROWS
    b = pl.program_id(0)
    nb = pl.num_programs(0)
    slot = lax.rem(b, 2)
    other = 1 - slot

    def row_in(idx_ref, r, sl):
        return pltpu.make_async_copy(h_hbm.at[idx_ref[0, 0, r]],
                                     xbuf.at[sl, r // SUBLANES, :, r % SUBLANES, :], sem_in.at[sl])

    def row_out(r, sl):
        return pltpu.make_async_copy(ybuf.at[sl, r // SUBLANES, :, r % SUBLANES, :],
                                     y_hbm.at[dst_ref[0, 0, r]], sem_out.at[sl])

    @pl.when(b == 0)
    def _():
        for r in range(R):
            row_in(src_ref, r, slot).start()

    for r in range(R):
        row_in(src_ref, r, slot).wait()
    for r in range(R):
        row_in(nsrc_ref, r, other).start()
    nchunk = xbuf.shape[2]
    lanes = xbuf.shape[4]
    xb = jnp.concatenate([xbuf[slot, :, c, :, :].reshape(R, lanes) for c in range(nchunk)],
                         axis=1).astype(BF16)
    gate = _dot(xb, wg_ref[0])
    up = _dot(xb, wu_ref[0])
    act = (_silu(gate) * up).astype(BF16)
    y = _dot(act, wd_ref[0]) * rw_ref[...]
    for c in range(nchunk):
        ybuf[slot, :, c, :, :] = y[:, c * lanes:(c + 1) * lanes].reshape(R // SUBLANES, SUBLANES, lanes)

    @pl.when(b >= 1)
    def _():
        for r in range(R):
            row_out(r, other).wait()

    for r in range(R):
        row_out(r, slot).start(priority=1)

    @pl.when(b == nb - 1)
    def _():
        for r in range(R):
            row_out(r, slot).wait()
        for r in range(R):
            row_in(nsrc_ref, r, other).wait()


def _moe(h2, row_src, row_dst, row_w, block_e, wg, wu, wd):
    n_tok, ts, tw = h2.shape
    D = ts * tw
    n_blocks = block_e.shape[0]
    R = MOE_ROWS
    F = wg.shape[-1]
    grid_spec = pltpu.PrefetchScalarGridSpec(
        num_scalar_prefetch=1,
        grid=(n_blocks,),
        in_specs=[pl.BlockSpec((1, 1, R), lambda b, be: (b, 0, 0), memory_space=pltpu.SMEM),
                  pl.BlockSpec((1, 1, R), lambda b, be: (jnp.minimum(b + 1, n_blocks - 1), 0, 0),
                               memory_space=pltpu.SMEM),
                  pl.BlockSpec((1, 1, R), lambda b, be: (b, 0, 0), memory_space=pltpu.SMEM),
                  pl.BlockSpec((R, 1), lambda b, be: (b, 0)),
                  pl.BlockSpec(memory_space=pl.ANY),
                  pl.BlockSpec((1, D, F), lambda b, be: (be[b], 0, 0)),
                  pl.BlockSpec((1, D, F), lambda b, be: (be[b], 0, 0)),
                  pl.BlockSpec((1, F, D), lambda b, be: (be[b], 0, 0))],
        out_specs=pl.BlockSpec(memory_space=pl.ANY),
        scratch_shapes=[pltpu.VMEM((2, R // SUBLANES, ts, SUBLANES, tw), F32),
                        pltpu.VMEM((2, R // SUBLANES, ts, SUBLANES, tw), F32),
                        pltpu.SemaphoreType.DMA((2,)), pltpu.SemaphoreType.DMA((2,))],
    )
    return pl.pallas_call(
        _moe_kernel,
        grid_spec=grid_spec,
        out_shape=jax.ShapeDtypeStruct((2 * n_tok + R, ts, tw), F32),
        compiler_params=_cparams(("arbitrary",)),
        name="moe",
    )(block_e, row_src, row_src, row_dst, row_w, h2, wg, wu, wd)


def _ffn_add_kernel(x_ref, m5b_ref, m5c_ref, y0_ref, y1_ref, o_ref, *, ctx_len):
    i = pl.program_id(1)
    n = x_ref.shape[1]
    row = i * n + lax.broadcasted_iota(I32, (n, 1), 0)
    m5 = jnp.where(row < ctx_len, m5c_ref[0], m5b_ref[0])
    o_ref[0] = x_ref[0] + m5 * (_load_token_tiles(y0_ref) + _load_token_tiles(y1_ref))


def _ffn_add(x, mods, ctx_row, y, ctx_len):
    B, S, D = x.shape
    nt = S // ROW_TILE
    tokd = pl.BlockSpec((1, ROW_TILE, D), lambda b, i: (b, i, 0))
    yspec = lambda slot: pl.BlockSpec((ROW_TILE, SUBLANES, D // SUBLANES),
                                      lambda b, i: (slot * B * nt + b * nt + i, 0, 0))
    return pl.pallas_call(
        functools.partial(_ffn_add_kernel, ctx_len=ctx_len),
        grid=(B, nt),
        in_specs=[tokd] + _mod_specs(2, ctx_row, (5,), D) + [yspec(0), yspec(1)],
        out_specs=tokd,
        out_shape=jax.ShapeDtypeStruct((B, S, D), F32),
        input_output_aliases={0: 0},
        compiler_params=_cparams(("parallel", "parallel")),
        name="ffn_add",
    )(x, mods, mods, y, y)


def _final_kernel(x_ref, m5_ref, y0_ref, y1_ref, g_ref, o_ref):
    x = x_ref[0] + m5_ref[0] * (_load_token_tiles(y0_ref) + _load_token_tiles(y1_ref))
    o_ref[0] = x * lax.rsqrt(jnp.mean(x * x, axis=-1, keepdims=True) + EPS) * g_ref[...]


def _final(x, mods, y, gain, ctx_len):
    B, S, D = x.shape
    nt = S // ROW_TILE
    skip = ctx_len // ROW_TILE
    tok_in = pl.BlockSpec((1, ROW_TILE, D), lambda b, i: (b, i + skip, 0))
    yspec = lambda slot: pl.BlockSpec((ROW_TILE, SUBLANES, D // SUBLANES),
                                      lambda b, i: (slot * B * nt + b * nt + i + skip, 0, 0))
    return pl.pallas_call(
        _final_kernel,
        grid=(B, nt - skip),
        in_specs=[tok_in, pl.BlockSpec((1, 1, D), lambda b, i: (b, 0, 5)), yspec(0), yspec(1),
                  pl.BlockSpec((1, D), lambda b, i: (0, 0))],
        out_specs=pl.BlockSpec((1, ROW_TILE, D), lambda b, i: (b, i, 0)),
        out_shape=jax.ShapeDtypeStruct((B, S - ctx_len, D), F32),
        compiler_params=_cparams(("parallel", "parallel")),
        name="final_norm",
    )(x, mods, y, y, gain.reshape(1, D))


def _pad_w_in(w):
    W = BRANCH_WIDTH
    a = w[:, :A_COLS]
    qkvz = w[:, A_COLS:A_COLS + 4 * W]
    ab = w[:, A_COLS + 4 * W:A_COLS + 4 * W + 4 * B_HEADS]
    rest = w[:, A_COLS + 4 * W + 4 * B_HEADS:]
    ab = jnp.pad(ab, ((0, 0), (0, LANES - 4 * B_HEADS)))
    out = jnp.concatenate([a, ab, qkvz, rest], axis=1).astype(BF16)
    assert out.shape[1] == N_IN_PAD
    return out


def kernel(x, c, ctx, c_ctx, w_ada, b_ada, norm_mix, norm_ffn, norm_final, w_in, rwkv_mu, rwkv_w0, rwkv_w2, rwkv_a0, rwkv_a2, rwkv_g2, rwkv_kk, rwkv_ka, rwkv_rk, rwkv_ln_g, rwkv_ln_b, gdn_conv, gdn_A_log, gdn_dt_bias, gdn_norm, conf_dw, conf_dw_b, conf_ln_g, conf_ln_b, w_branch, w_out, w_router, router_bias, w_e_gate, w_e_up, w_e_down):
    B, T, D = x.shape
    ctx_len = ctx.shape[1]
    depth = w_in.shape[0]
    assert ctx_len == ROW_TILE and T % ROW_TILE == 0 and T % GRID_W == 0
    S = ctx_len + T
    n_tok = B * S

    xs = jnp.concatenate([ctx, x], axis=1)
    rows = -(-(B + 1) // SUBLANES) * SUBLANES
    svec = jnp.concatenate([c, c_ctx[None, :], jnp.zeros((rows - B - 1, D), F32)], axis=0)
    mods_all = _ada(svec, w_ada, b_ada)
    ctx_row = B

    out = None
    for l in range(depth):
        mods = mods_all[l].reshape(rows, 1, 6 * D)
        P = _inproj(xs, norm_mix[l], mods, ctx_row, _pad_w_in(w_in[l]), ctx_len)
        prep = _rwkv_prep(P, rwkv_mu[l], rwkv_w0[l], rwkv_w2[l], rwkv_a0[l], rwkv_a2[l], rwkv_g2[l],
                          rwkv_kk[l], rwkv_ka[l], rwkv_rk[l])
        bonus, gate = prep[11], prep[12]
        y0, y1 = _rwkv_scan(prep[:11], ctx_len)
        q, k, v, gcol, grow = _gdn_prep(P, gdn_conv[l], gdn_A_log[l], gdn_dt_bias[l])
        o0, o1 = _gdn_scan(q, k, v, gcol, grow, ctx_len)
        ycv = _conformer_conv(P, conf_dw[l], ctx_len)
        xs = _merge(xs, mods, ctx_row, y0, y1, bonus, gate, o0, o1, P, ycv,
                    rwkv_ln_g[l], rwkv_ln_b[l], gdn_norm[l], conf_dw_b[l], conf_ln_g[l], conf_ln_b[l],
                    w_branch[l].astype(BF16), w_out[l].astype(BF16), ctx_len)
        h2, e, w = _ffn_pre(xs, norm_ffn[l], mods, ctx_row, w_router, router_bias, ctx_len)
        e = e.transpose(1, 0, 2).reshape(2, n_tok)
        w = w.transpose(1, 0, 2).reshape(2, n_tok)
        row_src, row_dst, row_w, block_e = _moe_dispatch(e, w, n_tok)
        y = _moe(h2, row_src, row_dst, row_w, block_e,
                 w_e_gate[l].astype(BF16), w_e_up[l].astype(BF16), w_e_down[l].astype(BF16))
        if l < depth - 1:
            xs = _ffn_add(xs, mods, ctx_row, y, ctx_len)
        else:
            out = _final(xs, mods, y, norm_final, ctx_len)
    return out
```

```python
import functools
import math

import jax
import jax.numpy as jnp
from jax import lax
from jax.experimental import pallas as pl
from jax.experimental.pallas import tpu as pltpu

F32 = jnp.float32
BF16 = jnp.bfloat16
I32 = jnp.int32
HIGHEST = lax.Precision.HIGHEST

EPS = 1e-6
LN_X_EPS = 64e-5
GRID_W = 64
BRANCH_WIDTH = 512
A_HEADS = 8
A_HEAD_DIM = 64
B_HEADS = 4
B_HEAD_DIM = 128
DECAY_LORA = 64
SHORT_CONV = 5
DW_CONV = 31
N_EXPERTS = 16
N_GROUPS = 4
EXP_PER_GROUP = 4
D_EXPERT = 512

ROW_TILE = 256
CHUNK = 64
RWKV_SCAN_CHUNKS = 4
MOE_ROWS = 128
LANES = 128
SUBLANES = 8
HALO_ROWS = 16
VMEM_LIMIT = 56 * 1024 * 1024

A_COLS = 1920
COL_A = 0
COL_AB = 1920
COL_Q = 2048
COL_Z = 3584
COL_C = 4096
COL_G = 5120
N_IN_PAD = 8192
LORA_W = 3 * LANES


def _cparams(sem):
    return pltpu.CompilerParams(dimension_semantics=sem, vmem_limit_bytes=VMEM_LIMIT)


def _sigmoid(x):
    return 0.5 * jnp.tanh(0.5 * x) + 0.5


def _silu(x):
    return x * _sigmoid(x)


def _softplus(x):
    return jnp.maximum(x, 0.0) + jnp.log(1.0 + jnp.exp(-jnp.abs(x)))


def _dot(a, b):
    return jnp.dot(a, b, preferred_element_type=F32)


def _dot_nt(a, b):
    return lax.dot_general(a, b, (((1,), (1,)), ((), ())), preferred_element_type=F32)


def _bf16_parts(x, n):
    parts = []
    r = x
    for _ in range(n):
        p = r.astype(BF16)
        parts.append(p)
        r = r - p.astype(F32)
    return parts


def _dot_sel_left(m, x, n=3):
    acc = None
    for p in _bf16_parts(x, n):
        t = _dot(m, p)
        acc = t if acc is None else acc + t
    return acc


def _dot_sel_right(x, m, n=2):
    acc = None
    for p in _bf16_parts(x, n):
        t = _dot(p, m)
        acc = t if acc is None else acc + t
    return acc


def _group_ones(n, group):
    r = lax.broadcasted_iota(I32, (n, n), 0) // group
    c = lax.broadcasted_iota(I32, (n, n), 1) // group
    return jnp.where(r == c, 1.0, 0.0).astype(BF16)


def _chunk_tri(n, lower):
    r = lax.broadcasted_iota(I32, (n, n), 0)
    c = lax.broadcasted_iota(I32, (n, n), 1)
    same = (r // CHUNK) == (c // CHUNK)
    keep = (c <= r) if lower else (c >= r)
    return jnp.where(same & keep, 1.0, 0.0).astype(BF16)


def _shift_down(x, prev_rows, s):
    n = x.shape[0]
    hr = prev_rows.shape[0]
    out = pltpu.roll(x, s, axis=0)
    row = lax.broadcasted_iota(I32, (n, 1), 0)
    for q in range(s):
        out = jnp.where(row == q, prev_rows[hr - s + q:hr - s + q + 1, :], out)
    return out


def _shift_up(x, next_rows, s):
    n = x.shape[0]
    out = pltpu.roll(x, n - s, axis=0)
    row = lax.broadcasted_iota(I32, (n, 1), 0)
    for q in range(s):
        out = jnp.where(row == n - s + q, next_rows[q:q + 1, :], out)
    return out


def _halo_rows(p_prev_ref, p_next_ref, i, n_tiles):
    prev_rows = p_prev_ref[0].astype(F32)
    next_rows = p_next_ref[0].astype(F32)
    prev_rows = jnp.where(i <= 1, 0.0, prev_rows)
    next_rows = jnp.where((i == 0) | (i == n_tiles - 1), 0.0, next_rows)
    return prev_rows, next_rows


def _halo_specs(width, col, tiles_per_batch):
    assert col % width == 0
    col_block = col // width
    sub_per_tile = ROW_TILE // HALO_ROWS
    last = tiles_per_batch * sub_per_tile - 1
    main = pl.BlockSpec((1, ROW_TILE, width), lambda b, i: (b, i, col_block))
    prev = pl.BlockSpec((1, HALO_ROWS, width),
                        lambda b, i: (b, jnp.maximum(i * sub_per_tile - 1, 0), col_block))
    nxt = pl.BlockSpec((1, HALO_ROWS, width),
                       lambda b, i: (b, jnp.minimum((i + 1) * sub_per_tile, last), col_block))
    return [main, prev, nxt]


def _ada_kernel(s_ref, w_ref, b_ref, o_ref):
    s = s_ref[...]
    o_ref[0] = jnp.dot(_silu(s), w_ref[0], preferred_element_type=F32, precision=HIGHEST) + b_ref[0]


def _ada(svec, w_ada, b_ada):
    L, D, N = w_ada.shape
    rows = svec.shape[0]
    tn = 1536
    return pl.pallas_call(
        _ada_kernel,
        grid=(L, N // tn),
        in_specs=[pl.BlockSpec((rows, D), lambda l, j: (0, 0)),
                  pl.BlockSpec((1, D, tn), lambda l, j: (l, 0, j)),
                  pl.BlockSpec((1, 1, tn), lambda l, j: (l, 0, j))],
        out_specs=pl.BlockSpec((1, rows, tn), lambda l, j: (l, 0, j)),
        out_shape=jax.ShapeDtypeStruct((L, rows, N), F32),
        compiler_params=_cparams(("parallel", "parallel")),
        name="ada",
    )(svec, w_ada, b_ada.reshape(L, 1, N))


def _store_token_tiles(ref, x):
    w = x.shape[1] // SUBLANES
    for s in range(SUBLANES):
        ref[:, s, :] = x[:, s * w:(s + 1) * w]


def _load_token_tiles(ref):
    return jnp.concatenate([ref[:, s, :] for s in range(SUBLANES)], axis=1)


def _modulated(x, gain, mods, row0, ctx_len):
    sh_b, sc_b, sh_c, sc_c = mods
    y = x * lax.rsqrt(jnp.mean(x * x, axis=-1, keepdims=True) + EPS) * gain
    row = row0 + lax.broadcasted_iota(I32, (x.shape[0], 1), 0)
    isc = row < ctx_len
    shift = jnp.where(isc, sh_c, sh_b)
    scale = jnp.where(isc, sc_c, sc_b)
    return y * (1.0 + scale) + shift


def _mod_specs(n_grid, ctx_row, chunks, d):
    def spec(row_of, c):
        if n_grid == 2:
            return pl.BlockSpec((1, 1, d), lambda b, i: (row_of(b), 0, c))
        return pl.BlockSpec((1, 1, d), lambda b, i, j: (row_of(b), 0, c))
    return ([spec(lambda b: b, c) for c in chunks] + [spec(lambda b: ctx_row, c) for c in chunks])


def _inproj_kernel(x_ref, g_ref, shb_ref, scb_ref, shc_ref, scc_ref, w_ref, o_ref, h_scr, *, tm, ctx_len):
    i = pl.program_id(1)
    j = pl.program_id(2)

    @pl.when(j == 0)
    def _():
        mods = (shb_ref[0], scb_ref[0], shc_ref[0], scc_ref[0])
        h_scr[...] = _modulated(x_ref[0], g_ref[...], mods, i * tm, ctx_len).astype(BF16)

    o_ref[0] = _dot(h_scr[...], w_ref[...]).astype(o_ref.dtype)


def _inproj(x, gain, mods, ctx_row, wp, ctx_len):
    B, S, D = x.shape
    N = wp.shape[1]
    tm = S // 2
    tn = 1024
    return pl.pallas_call(
        functools.partial(_inproj_kernel, tm=tm, ctx_len=ctx_len),
        grid=(B, S // tm, N // tn),
        in_specs=[pl.BlockSpec((1, tm, D), lambda b, i, j: (b, i, 0)),
                  pl.BlockSpec((1, D), lambda b, i, j: (0, 0))]
                 + _mod_specs(3, ctx_row, (0, 1), D)
                 + [pl.BlockSpec((D, tn), lambda b, i, j: (0, j))],
        out_specs=pl.BlockSpec((1, tm, tn), lambda b, i, j: (b, i, j)),
        out_shape=jax.ShapeDtypeStruct((B, S, N), BF16),
        scratch_shapes=[pltpu.VMEM((tm, D), BF16)],
        compiler_params=_cparams(("parallel", "parallel", "arbitrary")),
        name="inproj",
    )(x, gain.reshape(1, D), mods, mods, mods, mods, wp)


def _rwkv_prep_kernel(r_ref, rp_ref, rn_ref, k_ref, kp_ref_, kn_ref, vv_ref, vp_ref, vn_ref,
                      x_ref, xp_ref, xn_ref,
                      mu_ref, w0_ref, w2_ref, a0_ref, a2_ref, g2_ref, kk_ref, ka_ref, rk_ref,
                      rt0_ref, kp0_ref, bt0_ref, kt0_ref, pc0_ref,
                      rt1_ref, kp1_ref, bt1_ref, kt1_ref, pc1_ref,
                      v_ref, bonus_ref, g_ref, *, n_tiles):
    i = pl.program_id(1)
    W = BRANCH_WIDTH

    def shifted(main, prv, nxt, col, width):
        p = main[0].astype(F32)
        prev_rows, next_rows = _halo_rows(prv, nxt, i, n_tiles)
        mu = mu_ref[:, col:col + width]
        return (p + mu[0:1] * (_shift_down(p, prev_rows, 1) - p)
                + mu[1:2] * (_shift_up(p, next_rows, 1) - p))

    r = shifted(r_ref, rp_ref, rn_ref, 0, W)
    k = shifted(k_ref, kp_ref_, kn_ref, W, W)
    v = shifted(vv_ref, vp_ref, vn_ref, 2 * W, W)
    lora = shifted(x_ref, xp_ref, xn_ref, 3 * W, LORA_W)
    xw = jnp.tanh(lora[:, 0:LANES])
    xa = lora[:, LANES:2 * LANES]
    xg = _sigmoid(lora[:, 2 * LANES:3 * LANES])
    lane_hi = lax.broadcasted_iota(I32, xw.shape, 1) >= DECAY_LORA

    g_ref[0] = _dot(xg.astype(BF16), g2_ref[...].astype(BF16)).astype(g_ref.dtype)
    v_ref[0] = v.astype(BF16)

    gsum = _group_ones(W, A_HEAD_DIM)
    kx = k * kk_ref[...]
    kk = kx * lax.rsqrt(_dot_sel_right(kx * kx, gsum) + EPS)
    bonus_ref[0] = (_dot_sel_right(r * k * rk_ref[...], gsum) * v).astype(bonus_ref.dtype)

    w2 = w2_ref[...].astype(BF16)
    a2 = a2_ref[...].astype(BF16)
    n = r.shape[0]
    outs = ((rt0_ref, kp0_ref, bt0_ref, kt0_ref, pc0_ref), (rt1_ref, kp1_ref, bt1_ref, kt1_ref, pc1_ref))
    for d in range(2):
        keep = lane_hi if d == 1 else jnp.logical_not(lane_hi)
        lw = w0_ref[d:d + 1, :] + _dot(jnp.where(keep, xw, 0.0).astype(BF16), w2)
        ew = _sigmoid(lw) * math.exp(-0.5)
        a = _sigmoid(a0_ref[d:d + 1, :] + _dot(jnp.where(keep, xa, 0.0).astype(BF16), a2))
        keff = k * (1.0 + (a - 1.0) * ka_ref[...])
        cs = _dot_sel_left(_chunk_tri(n, lower=(d == 0)), ew)
        p_incl = jnp.exp(-cs)
        p_excl = jnp.exp(ew - cs)
        p_inv = jnp.exp(cs)
        rt_ref, kp_ref, bt_ref, kt_ref, pc_ref = outs[d]
        rt_ref[0] = (r * p_incl).astype(BF16)
        kp_ref[0] = (kk * p_excl).astype(BF16)
        bt_ref[0] = (kk * a * p_inv).astype(BF16)
        kt_ref[0] = (keff * p_inv).astype(BF16)
        for c in range(n // CHUNK):
            row = c * CHUNK + (CHUNK - 1 if d == 0 else 0)
            pc_ref[0, c] = p_incl[row:row + 1, :]


def _rwkv_prep(P, mu, w0, w2, a0, a2, g2, k_k, k_a, r_k):
    B, S, _ = P.shape
    W = BRANCH_WIDTH
    nt = S // ROW_TILE
    cpt = ROW_TILE // CHUNK
    halo = (_halo_specs(W, COL_A, nt) + _halo_specs(W, COL_A + W, nt) + _halo_specs(W, COL_A + 2 * W, nt)
            + _halo_specs(LORA_W, COL_A + 3 * W, nt))
    full = lambda shape: pl.BlockSpec(shape, lambda b, i: tuple(0 for _ in shape))
    tok = pl.BlockSpec((1, ROW_TILE, W), lambda b, i: (b, i, 0))
    pcs = pl.BlockSpec((1, cpt, 1, W), lambda b, i: (b, i, 0, 0))
    bf = jax.ShapeDtypeStruct((B, S, W), BF16)
    f32 = jax.ShapeDtypeStruct((B, S, W), F32)
    pcshape = jax.ShapeDtypeStruct((B, S // CHUNK, 1, W), F32)
    return pl.pallas_call(
        functools.partial(_rwkv_prep_kernel, n_tiles=nt),
        grid=(B, nt),
        in_specs=halo + [full((2, A_COLS)), full((2, W)), full((2 * DECAY_LORA, W)), full((2, W)),
                         full((2 * DECAY_LORA, W)), full((LANES, W)), full((1, W)), full((1, W)),
                         full((1, W))],
        out_specs=[tok, tok, tok, tok, pcs, tok, tok, tok, tok, pcs, tok, tok, tok],
        out_shape=[bf, bf, bf, bf, pcshape, bf, bf, bf, bf, pcshape, bf, bf, bf],
        compiler_params=_cparams(("parallel", "parallel")),
        name="rwkv_prep",
    )(*([P] * 12), mu, w0, w2.reshape(2 * DECAY_LORA, W), a0, a2.reshape(2 * DECAY_LORA, W), g2,
      k_k.reshape(1, W), k_a.reshape(1, W), r_k.reshape(1, W))


def _unit_lower_inverse(a_neg, steps):
    n = a_neg.shape[0]
    eye = jnp.where(lax.broadcasted_iota(I32, (n, n), 0) == lax.broadcasted_iota(I32, (n, n), 1), 1.0, 0.0)
    t = eye + a_neg
    xp = a_neg.astype(BF16)
    for _ in range(steps - 1):
        x2 = _dot(xp, xp)
        xp = x2.astype(BF16)
        t = t + _dot(t.astype(BF16), xp)
    return t


def _run_chains(chains):
    while chains:
        alive = []
        for g in chains:
            try:
                next(g)
                alive.append(g)
            except StopIteration:
                pass
        chains = alive


def _stack_heads(x, lane_hi):
    z = jnp.zeros_like(x)
    return jnp.concatenate([jnp.where(lane_hi, z, x), jnp.where(lane_hi, x, z)], axis=0)


def _rwkv_scan_kernel(rt0, kp0, bt0, kt0, v0, pc0, rt1, kp1, bt1, kt1, v1, pc1, y0_ref, y1_ref, st_ref):
    s = pl.program_id(1)

    @pl.when(s == 0)
    def _():
        st_ref[...] = jnp.zeros_like(st_ref)

    C = CHUNK
    n2 = 2 * C
    ri = lax.broadcasted_iota(I32, (n2, n2), 0)
    ci = lax.broadcasted_iota(I32, (n2, n2), 1)
    same = (ri // C) == (ci // C)
    tr = ri % C
    ts = ci % C
    lane_hi = lax.broadcasted_iota(I32, (C, n2), 1) >= C
    eye = jnp.where(ri == ci, 1.0, 0.0)
    dirs = ((rt0, kp0, bt0, kt0, v0, pc0, y0_ref), (rt1, kp1, bt1, kt1, v1, pc1, y1_ref))
    nch = rt0.shape[1] // C
    ready = {}

    def intra(d, j, cc):
        rt, kp, bt, kt, vv, pc, _ = dirs[d]
        strict = same & ((ts < tr) if d == 0 else (ts > tr))
        incl = same & ((ts <= tr) if d == 0 else (ts >= tr))
        sl = slice(n2 * j, n2 * (j + 1))
        rows = slice(cc * C, (cc + 1) * C)
        kap = kp[0, rows, sl]
        r_ = rt[0, rows, sl]
        b_ = bt[0, rows, sl]
        k_ = kt[0, rows, sl]
        v_st = _stack_heads(vv[0, rows, sl], lane_hi)
        aa = _dot_nt(jnp.concatenate([_stack_heads(kap, lane_hi), _stack_heads(r_, lane_hi)], axis=0),
                     jnp.concatenate([b_, b_, k_, k_], axis=0))
        yield
        x = jnp.where(strict, -aa[:n2, :n2], 0.0)
        a_ak = jnp.where(strict, aa[:n2, n2:], 0.0).astype(BF16)
        a_rb = jnp.where(incl, aa[n2:, :n2], 0.0).astype(BF16)
        a_rk = jnp.where(incl, aa[n2:, n2:], 0.0).astype(BF16)
        t = eye + x
        xp = x.astype(BF16)
        b_t = _stack_heads(b_, lane_hi).astype(F32).T.astype(BF16)
        k_t = _stack_heads(k_, lane_hi).astype(F32).T.astype(BF16)
        sv = _dot(jnp.concatenate([a_ak, a_rk, k_t], axis=0), v_st)
        akv = sv[:n2]
        arkv = sv[n2:2 * n2]
        kv0 = sv[2 * n2:]
        x2 = _dot(xp, xp)
        yield
        for _ in range(4):
            xp = x2.astype(BF16)
            tx = _dot(jnp.concatenate([t.astype(BF16), xp], axis=0), xp)
            yield
            t = t + tx[:n2]
            x2 = tx[n2:]
        t = t + _dot(t.astype(BF16), x2.astype(BF16))
        yield
        pcm = jnp.broadcast_to(pc[0, cc][:, sl], (n2, n2)).T
        ready[(d, j, cc)] = (jnp.concatenate([kap, r_], axis=0), t.astype(BF16), akv, arkv, kv0,
                             jnp.concatenate([a_rb, b_t], axis=0), pcm)

    def state(d, j):
        y_ref = dirs[d][6]
        sl = slice(n2 * j, n2 * (j + 1))
        h = st_ref[d, j]
        for cc in (range(nch) if d == 0 else range(nch - 1, -1, -1)):
            while (d, j, cc) not in ready:
                yield
            kap_r, t_b, akv, arkv, kv0, arb_bt, pcm = ready.pop((d, j, cc))
            krh = _dot(kap_r, h.astype(BF16))
            yield
            rhs = akv + _stack_heads(krh[:C], lane_hi)
            u_st = -_dot(t_b, rhs.astype(BF16))
            yield
            yd = _dot(arb_bt, u_st.astype(BF16))
            yield
            y_st = yd[:n2] + arkv
            y_ref[0, cc * C:(cc + 1) * C, sl] = (y_st[:C] + y_st[C:] + krh[C:]).astype(y_ref.dtype)
            h = pcm * (h + kv0 + yd[n2:])
        st_ref[d, j] = h

    units = [(d, j) for d in range(2) for j in range(BRANCH_WIDTH // n2)]
    chains = []
    for pos in range(nch):
        for d, j in units:
            chains.append(intra(d, j, pos if d == 0 else nch - 1 - pos))
    _run_chains(chains + [state(d, j) for d, j in units])


def _scan_tile_maps(nt_ctx, nt):
    def t0(s):
        return s

    def t1(s):
        return jnp.where(s < nt_ctx, nt_ctx - 1 - s, nt + nt_ctx - 1 - s)

    return t0, t1


def _rwkv_scan(prep, ctx_len):
    rt0, kp0, bt0, kt0, pc0, rt1, kp1, bt1, kt1, pc1, v = prep
    B, S, W = v.shape
    rows = RWKV_SCAN_CHUNKS * CHUNK
    c0, c1 = _scan_tile_maps(ctx_len // rows, S // rows)
    tok = lambda cm: pl.BlockSpec((1, rows, W), lambda b, s: (b, cm(s), 0))
    pcs = lambda cm: pl.BlockSpec((1, RWKV_SCAN_CHUNKS, 1, W), lambda b, s: (b, cm(s), 0, 0))
    y = jax.ShapeDtypeStruct((B, S, W), BF16)
    return pl.pallas_call(
        _rwkv_scan_kernel,
        grid=(B, S // rows),
        in_specs=[tok(c0), tok(c0), tok(c0), tok(c0), tok(c0), pcs(c0),
                  tok(c1), tok(c1), tok(c1), tok(c1), tok(c1), pcs(c1)],
        out_specs=[tok(c0), tok(c1)],
        out_shape=[y, y],
        scratch_shapes=[pltpu.VMEM((2, W // (2 * CHUNK), 2 * CHUNK, 2 * CHUNK), F32)],
        compiler_params=_cparams(("parallel", "arbitrary")),
        name="rwkv_scan",
    )(rt0, kp0, bt0, kt0, v, pc0, rt1, kp1, bt1, kt1, v, pc1)


def _gdn_prep_kernel(qi_ref, qp_ref, qn_ref, ki_ref, kp_ref, kn_ref, vi_ref, vp_ref, vn_ref,
                     ab_ref, cw_ref, al_ref, dt_ref,
                     q_ref, k_ref, v_ref, gcol_ref, grow_ref, *, n_tiles):
    i = pl.program_id(1)
    W = BRANCH_WIDTH
    half = SHORT_CONV // 2

    def conv_silu(main, prv, nxt, col):
        x = main[0].astype(F32)
        prev_rows, next_rows = _halo_rows(prv, nxt, i, n_tiles)
        cw = cw_ref[:, col:col + W]
        acc = cw[half:half + 1] * x
        for s in range(1, half + 1):
            acc = acc + cw[half - s:half - s + 1] * _shift_down(x, prev_rows, s)
            acc = acc + cw[half + s:half + s + 1] * _shift_up(x, next_rows, s)
        return _silu(acc)

    yq = conv_silu(qi_ref, qp_ref, qn_ref, 0)
    yk = conv_silu(ki_ref, kp_ref, kn_ref, W)
    v_ref[0] = conv_silu(vi_ref, vp_ref, vn_ref, 2 * W).astype(v_ref.dtype)
    for hd in range(B_HEADS):
        sl = slice(hd * B_HEAD_DIM, (hd + 1) * B_HEAD_DIM)
        q = yq[:, sl]
        q_ref[0, :, sl] = (q * lax.rsqrt(jnp.sum(q * q, axis=-1, keepdims=True) + EPS)
                           * (B_HEAD_DIM ** -0.5)).astype(q_ref.dtype)
        k = yk[:, sl]
        k_ref[0, :, sl] = (k * lax.rsqrt(jnp.sum(k * k, axis=-1, keepdims=True) + EPS)).astype(k_ref.dtype)

    ab = ab_ref[0].astype(F32)
    n = ab.shape[0]
    lane = lax.broadcasted_iota(I32, ab.shape, 1)
    g = -jnp.exp(al_ref[...]) * _softplus(ab + dt_ref[...])
    g = jnp.where(lane < 2 * B_HEADS, g, 0.0)
    gc_f = _dot_sel_left(_chunk_tri(n, lower=True), g)
    gc_b = _dot_sel_left(_chunk_tri(n, lower=False), g)
    gc = jnp.where(lane < B_HEADS, gc_f, gc_b)
    beta = _sigmoid(ab)
    out = jnp.where(lane < 2 * B_HEADS, gc, jnp.where(lane < 4 * B_HEADS, beta, 0.0))
    gcol_ref[0] = out
    grow_ref[0] = out.T[:4 * B_HEADS, :]


def _gdn_prep(P, conv_w, A_log, dt_bias):
    B, S, _ = P.shape
    W = BRANCH_WIDTH
    nt = S // ROW_TILE
    halo = _halo_specs(W, COL_Q, nt) + _halo_specs(W, COL_Q + W, nt) + _halo_specs(W, COL_Q + 2 * W, nt)
    pad = jnp.zeros((1, LANES - 2 * B_HEADS), F32)
    al = jnp.concatenate([A_log.reshape(1, 2 * B_HEADS), pad], axis=1)
    dt = jnp.concatenate([dt_bias.reshape(1, 2 * B_HEADS), pad], axis=1)
    full = lambda shape: pl.BlockSpec(shape, lambda b, i: tuple(0 for _ in shape))
    tok = pl.BlockSpec((1, ROW_TILE, W), lambda b, i: (b, i, 0))
    bf16 = jax.ShapeDtypeStruct((B, S, W), BF16)
    return pl.pallas_call(
        functools.partial(_gdn_prep_kernel, n_tiles=nt),
        grid=(B, nt),
        in_specs=halo + [pl.BlockSpec((1, ROW_TILE, LANES), lambda b, i: (b, i, COL_AB // LANES)),
                         full((SHORT_CONV, 3 * W)), full((1, LANES)), full((1, LANES))],
        out_specs=[tok, tok, tok,
                   pl.BlockSpec((1, ROW_TILE, LANES), lambda b, i: (b, i, 0)),
                   pl.BlockSpec((1, 4 * B_HEADS, ROW_TILE), lambda b, i: (b, 0, i))],
        out_shape=[bf16, bf16, bf16, jax.ShapeDtypeStruct((B, S, LANES), F32),
                   jax.ShapeDtypeStruct((B, 4 * B_HEADS, S), F32)],
        compiler_params=_cparams(("parallel", "parallel")),
        name="gdn_prep",
    )(*([P] * 10), conv_w, al, dt)


def _gdn_scan_kernel(q0, k0, v0, gc0, gr0, q1, k1, v1, gc1, gr1, o0_ref, o1_ref, st_ref):
    s = pl.program_id(1)

    @pl.when(s == 0)
    def _():
        st_ref[...] = jnp.zeros_like(st_ref)

    C = CHUNK
    nch = q0.shape[1] // C
    ri = lax.broadcasted_iota(I32, (C, C), 0)
    ci = lax.broadcasted_iota(I32, (C, C), 1)
    dirs = ((q0, k0, v0, gc0, gr0, o0_ref), (q1, k1, v1, gc1, gr1, o1_ref))
    eye = jnp.where(ri == ci, 1.0, 0.0)
    ready = {}

    def intra(d, hd, cc):
        qr, kr, vr, gcr, grr, _ = dirs[d]
        incl = (ci <= ri) if d == 0 else (ci >= ri)
        strict = (ci < ri) if d == 0 else (ci > ri)
        last = C - 1 if d == 0 else 0
        rows = slice(cc * C, (cc + 1) * C)
        gcol = gcr[0, rows, :]
        grow = grr[0, cc]
        sl = slice(hd * B_HEAD_DIM, (hd + 1) * B_HEAD_DIM)
        idx = d * B_HEADS + hd
        q = qr[0, rows, sl].astype(F32)
        k = kr[0, rows, sl].astype(F32)
        v = vr[0, rows, sl].astype(F32)
        gc_c = gcol[:, idx:idx + 1]
        gc_r = grow[idx:idx + 1, :]
        beta = gcol[:, 2 * B_HEADS + idx:2 * B_HEADS + idx + 1]
        gc_last = gc_r[:, last:last + 1]
        diff = gc_c - gc_r
        decay_incl = jnp.where(incl, jnp.exp(jnp.where(incl, diff, 0.0)), 0.0)
        decay_strict = jnp.where(strict, decay_incl, 0.0)
        kb = k * beta
        e_gc = jnp.exp(gc_c)
        qk = _dot_nt(jnp.concatenate([kb, q], axis=0).astype(BF16), k.astype(BF16))
        yield
        x = -(qk[:C] * decay_strict)
        attn = (qk[C:] * decay_incl).astype(BF16)
        rhs = jnp.concatenate([v * beta, kb * e_gc], axis=1).astype(BF16)
        q_dec = (q * e_gc).astype(BF16)
        k_dec_t = (k * jnp.exp(gc_last - gc_c)).T.astype(BF16)
        g_last = jnp.exp(gc_last)
        t = eye + x
        xp = x.astype(BF16)
        x2 = _dot(xp, xp)
        yield
        for _ in range(4):
            xp = x2.astype(BF16)
            tx = _dot(jnp.concatenate([t.astype(BF16), xp], axis=0), xp)
            yield
            t = t + tx[:C]
            x2 = tx[C:]
        t = t + _dot(t.astype(BF16), x2.astype(BF16))
        yield
        sol = _dot(t.astype(BF16), rhs)
        yield
        ready[(d, hd, cc)] = (sol[:, :B_HEAD_DIM], sol[:, B_HEAD_DIM:].astype(BF16), q_dec, attn, k_dec_t,
                              g_last)

    def state(d, hd):
        o_ref = dirs[d][5]
        sl = slice(hd * B_HEAD_DIM, (hd + 1) * B_HEAD_DIM)
        st = st_ref[d, hd]
        for cc in (range(nch) if d == 0 else range(nch - 1, -1, -1)):
            while (d, hd, cc) not in ready:
                yield
            u, w, q_dec, attn, k_dec_t, g_last = ready.pop((d, hd, cc))
            ws_qs = _dot(jnp.concatenate([w, q_dec], axis=0), st.astype(BF16))
            yield
            vn_b = (u - ws_qs[:C]).astype(BF16)
            od = _dot(jnp.concatenate([attn, k_dec_t], axis=0), vn_b)
            yield
            o_ref[0, cc * C:(cc + 1) * C, sl] = (ws_qs[C:] + od[:C]).astype(o_ref.dtype)
            st = st * g_last + od[C:]
        st_ref[d, hd] = st

    units = [(d, hd) for d in range(2) for hd in range(B_HEADS)]
    chains = []
    for pos in range(nch):
        for d, hd in units:
            chains.append(intra(d, hd, pos if d == 0 else nch - 1 - pos))
    _run_chains(chains + [state(d, hd) for d, hd in units])


def _gdn_scan(q, k, v, gcol, grow, ctx_len):
    B, S, W = q.shape
    nc = S // CHUNK
    nt = S // ROW_TILE
    cpt = ROW_TILE // CHUNK
    c0, c1 = _scan_tile_maps(ctx_len // ROW_TILE, nt)
    tok = lambda cm: pl.BlockSpec((1, ROW_TILE, W), lambda b, s: (b, cm(s), 0))
    gcs = lambda cm: pl.BlockSpec((1, ROW_TILE, LANES), lambda b, s: (b, cm(s), 0))
    grs = lambda cm: pl.BlockSpec((1, cpt, 4 * B_HEADS, CHUNK), lambda b, s: (b, cm(s), 0, 0))
    grow_c = grow.reshape(B, 4 * B_HEADS, nc, CHUNK).transpose(0, 2, 1, 3)
    o = jax.ShapeDtypeStruct((B, S, W), BF16)
    return pl.pallas_call(
        _gdn_scan_kernel,
        grid=(B, nt),
        in_specs=[tok(c0), tok(c0), tok(c0), gcs(c0), grs(c0), tok(c1), tok(c1), tok(c1), gcs(c1), grs(c1)],
        out_specs=[tok(c0), tok(c1)],
        out_shape=[o, o],
        scratch_shapes=[pltpu.VMEM((2, B_HEADS, B_HEAD_DIM, B_HEAD_DIM), F32)],
        compiler_params=_cparams(("parallel", "arbitrary")),
        name="gdn_scan",
    )(q, k, v, gcol, grow_c, q, k, v, gcol, grow_c)


def _conf_kernel(val_ref, gate_ref, dw_ref, o_ref, ctx_scr, w_scr, h_scr, *, ctx_len, n_rows):
    cblk = pl.program_id(1)
    K = DW_CONV
    half = K // 2
    pad = 2 * SUBLANES
    stride = GRID_W + 2 * pad
    u = val_ref[0].astype(F32) * _sigmoid(gate_ref[0].astype(F32))
    dw = dw_ref[...]

    ctx_scr[...] = jnp.zeros_like(ctx_scr)
    ctx_scr[pad:pad + ctx_len, :] = u[:ctx_len]
    acc = jnp.zeros((ctx_len, LANES), F32)
    for j in range(K):
        acc = acc + dw[j:j + 1] * ctx_scr[pad - half + j:pad - half + j + ctx_len, :]
    o_ref[0, :ctx_len, :] = acc.astype(o_ref.dtype)

    lat = u[ctx_len:]

    @pl.when(cblk < 2)
    def _():
        w_scr[...] = jnp.zeros_like(w_scr)
        for r in range(n_rows):
            w_scr[r * stride + pad:r * stride + pad + GRID_W, :] = lat[r * GRID_W:(r + 1) * GRID_W]
        for r in range(n_rows):
            a = jnp.zeros((GRID_W, LANES), F32)
            for j in range(K):
                o = r * stride + pad - half + j
                a = a + dw[j:j + 1] * w_scr[o:o + GRID_W, :]
            o_ref[0, ctx_len + r * GRID_W:ctx_len + (r + 1) * GRID_W, :] = a.astype(o_ref.dtype)

    @pl.when(cblk >= 2)
    def _():
        hp = half * GRID_W
        h_scr[...] = jnp.zeros_like(h_scr)
        h_scr[hp:hp + n_rows * GRID_W, :] = lat
        for r in range(n_rows):
            a = jnp.zeros((GRID_W, LANES), F32)
            for j in range(K):
                rr = r + j - half
                if 0 <= rr < n_rows:
                    a = a + dw[j:j + 1] * h_scr[hp + rr * GRID_W:hp + (rr + 1) * GRID_W, :]
            o_ref[0, ctx_len + r * GRID_W:ctx_len + (r + 1) * GRID_W, :] = a.astype(o_ref.dtype)


def _conformer_conv(P, dw, ctx_len):
    B, S, _ = P.shape
    W = BRANCH_WIDTH
    n_rows = (S - ctx_len) // GRID_W
    pad = 2 * SUBLANES
    nblk = W // LANES
    return pl.pallas_call(
        functools.partial(_conf_kernel, ctx_len=ctx_len, n_rows=n_rows),
        grid=(B, nblk),
        in_specs=[pl.BlockSpec((1, S, LANES), lambda b, c: (b, 0, COL_C // LANES + c)),
                  pl.BlockSpec((1, S, LANES), lambda b, c: (b, 0, (COL_C + W) // LANES + c)),
                  pl.BlockSpec((DW_CONV, LANES), lambda b, c: (0, c))],
        out_specs=pl.BlockSpec((1, S, LANES), lambda b, c: (b, 0, c)),
        out_shape=jax.ShapeDtypeStruct((B, S, W), BF16),
        scratch_shapes=[pltpu.VMEM((ctx_len + 2 * pad, LANES), F32),
                        pltpu.VMEM((n_rows * (GRID_W + 2 * pad), LANES), F32),
                        pltpu.VMEM(((n_rows + 2 * (DW_CONV // 2)) * GRID_W, LANES), F32)],
        compiler_params=_cparams(("parallel", "parallel")),
        name="conformer_conv",
    )(P, P, dw)


def _merge_kernel(x_ref, m2b_ref, m2c_ref, y0_ref, y1_ref, bonus_ref, g_ref, o0_ref, o1_ref, z_ref,
                  yc_ref, pg0_ref, pg1_ref, pg2_ref, lng_ref, lnb_ref, gng_ref, dwb_ref, cg_ref, cb_ref,
                  wb_ref, wo_ref, out_ref, *, ctx_len, skip):
    i = pl.program_id(1) + skip
    W = BRANCH_WIDTH
    D = x_ref.shape[-1]
    n = x_ref.shape[1]
    y = y0_ref[0].astype(F32) + y1_ref[0].astype(F32)
    gsum = _group_ones(W, A_HEAD_DIM)
    mu = _dot_sel_right(y, gsum) * (1.0 / A_HEAD_DIM)
    yc = y - mu
    var = _dot_sel_right(yc * yc, gsum) * (1.0 / A_HEAD_DIM)
    ya = (yc * lax.rsqrt(var + LN_X_EPS) * lng_ref[...] + lnb_ref[...] + bonus_ref[0].astype(F32)) * g_ref[0].astype(F32)
    o = o0_ref[0].astype(F32) + o1_ref[0].astype(F32)
    z = z_ref[0].astype(F32)
    parts = []
    for hd in range(B_HEADS):
        oh = o[:, hd * B_HEAD_DIM:(hd + 1) * B_HEAD_DIM]
        parts.append(oh * lax.rsqrt(jnp.mean(oh * oh, axis=-1, keepdims=True) + EPS) * gng_ref[...])
    yb = jnp.concatenate(parts, axis=1) * _silu(z)
    c = yc_ref[0].astype(F32) + dwb_ref[...]
    cm = jnp.mean(c, axis=-1, keepdims=True)
    cc = c - cm
    cv = jnp.mean(cc * cc, axis=-1, keepdims=True)
    ycf = _silu(cc * lax.rsqrt(cv + EPS) * cg_ref[...] + cb_ref[...])

    acc = jnp.zeros((n, D), F32)
    for nb, (br, pg_ref) in enumerate(((ya, pg0_ref), (yb, pg1_ref), (ycf, pg2_ref))):
        up = _dot(br.astype(BF16), wb_ref[nb])
        acc = acc + _sigmoid(pg_ref[0].astype(F32)) * up
    mix = _dot(acc.astype(BF16), wo_ref[...])
    row = i * n + lax.broadcasted_iota(I32, (n, 1), 0)
    m2 = jnp.where(row < ctx_len, m2c_ref[0], m2b_ref[0])
    out_ref[0] = x_ref[0] + m2 * mix


def _merge(x, mods, ctx_row, y0, y1, bonus, g, o0, o1, P, ycv, ln_g, ln_b, gdn_g, dw_b, c_g, c_b, wb, wo,
           ctx_len, latent_only):
    B, S, D = x.shape
    W = BRANCH_WIDTH
    nt = S // ROW_TILE
    skip = ctx_len // ROW_TILE if latent_only else 0
    assert COL_Z % W == 0 and COL_G % D == 0
    tokw = pl.BlockSpec((1, ROW_TILE, W), lambda b, i: (b, i + skip, 0))
    tokd = pl.BlockSpec((1, ROW_TILE, D), lambda b, i: (b, i + skip, 0))
    gate = lambda nb: pl.BlockSpec((1, ROW_TILE, D), lambda b, i: (b, i + skip, COL_G // D + nb))
    full = lambda shape: pl.BlockSpec(shape, lambda b, i: tuple(0 for _ in shape))
    return pl.pallas_call(
        functools.partial(_merge_kernel, ctx_len=ctx_len, skip=skip),
        grid=(B, nt - skip),
        in_specs=[tokd] + _mod_specs(2, ctx_row, (2,), D)
                 + [tokw, tokw, tokw, tokw, tokw, tokw,
                    pl.BlockSpec((1, ROW_TILE, W), lambda b, i: (b, i + skip, COL_Z // W)),
                    tokw, gate(0), gate(1), gate(2),
                    full((1, W)), full((1, W)), full((1, B_HEAD_DIM)), full((1, W)), full((1, W)),
                    full((1, W)), full((3, W, D)), full((D, D))],
        out_specs=pl.BlockSpec((1, ROW_TILE, D), lambda b, i: (b, i, 0)) if latent_only else tokd,
        out_shape=jax.ShapeDtypeStruct((B, S - skip * ROW_TILE, D), F32),
        input_output_aliases={} if latent_only else {0: 0},
        compiler_params=_cparams(("parallel", "parallel")),
        name="merge",
    )(x, mods, mods, y0, y1, bonus, g, o0, o1, P, ycv, P, P, P,
      ln_g.reshape(1, W), ln_b.reshape(1, W), gdn_g.reshape(1, B_HEAD_DIM), dw_b.reshape(1, W),
      c_g.reshape(1, W), c_b.reshape(1, W), wb, wo)


def _ffn_pre_kernel(x_ref, g_ref, shb_ref, scb_ref, shc_ref, scc_ref, wr_ref, rb_ref, h_ref, e_ref, w_ref,
                    *, ctx_len):
    i = pl.program_id(1)
    n = x_ref.shape[1]
    mods = (shb_ref[0], scb_ref[0], shc_ref[0], scc_ref[0])
    h = _modulated(x_ref[0], g_ref[...], mods, i * n, ctx_len)
    _store_token_tiles(h_ref, h)
    logits = lax.dot_general(wr_ref[...], h, (((1,), (1,)), ((), ())),
                             preferred_element_type=F32, precision=HIGHEST)
    scores = 1.0 / (1.0 + jnp.exp(-logits))
    sel = scores + rb_ref[...]
    rows = [sel[e:e + 1, :] for e in range(N_EXPERTS)]
    srow = [scores[e:e + 1, :] for e in range(N_EXPERTS)]
    gscore = []
    for gi in range(N_GROUPS):
        m = rows[gi * EXP_PER_GROUP:(gi + 1) * EXP_PER_GROUP]
        best = None
        for a in range(EXP_PER_GROUP):
            for b in range(a + 1, EXP_PER_GROUP):
                t = m[a] + m[b]
                best = t if best is None else jnp.maximum(best, t)
        gscore.append(best)
    grp = jnp.zeros_like(gscore[0], dtype=I32)
    gbest = gscore[0]
    for gi in range(1, N_GROUPS):
        better = gscore[gi] > gbest
        grp = jnp.where(better, gi, grp)
        gbest = jnp.where(better, gscore[gi], gbest)
    mem = []
    mem_s = []
    for a in range(EXP_PER_GROUP):
        va = rows[a]
        sa = srow[a]
        for gi in range(1, N_GROUPS):
            va = jnp.where(grp == gi, rows[gi * EXP_PER_GROUP + a], va)
            sa = jnp.where(grp == gi, srow[gi * EXP_PER_GROUP + a], sa)
        mem.append(va)
        mem_s.append(sa)
    neg = jnp.full_like(mem[0], -jnp.inf)
    i1 = jnp.zeros_like(grp)
    b1 = mem[0]
    for a in range(1, EXP_PER_GROUP):
        better = mem[a] > b1
        i1 = jnp.where(better, a, i1)
        b1 = jnp.where(better, mem[a], b1)
    i2 = jnp.zeros_like(grp)
    b2 = neg
    first = True
    for a in range(EXP_PER_GROUP):
        cand = jnp.where(i1 == a, neg, mem[a])
        if first:
            b2 = cand
            first = False
        else:
            better = cand > b2
            i2 = jnp.where(better, a, i2)
            b2 = jnp.where(better, cand, b2)
    s1 = mem_s[0]
    s2 = mem_s[0]
    for a in range(1, EXP_PER_GROUP):
        s1 = jnp.where(i1 == a, mem_s[a], s1)
        s2 = jnp.where(i2 == a, mem_s[a], s2)
    tot = s1 + s2
    e_ref[0, 0:1, :] = grp * EXP_PER_GROUP + i1
    e_ref[0, 1:2, :] = grp * EXP_PER_GROUP + i2
    w_ref[0, 0:1, :] = s1 / tot
    w_ref[0, 1:2, :] = s2 / tot


def _ffn_pre(x, gain, mods, ctx_row, w_router, router_bias, ctx_len):
    B, S, D = x.shape
    nt = S // ROW_TILE
    tokd = pl.BlockSpec((1, ROW_TILE, D), lambda b, i: (b, i, 0))
    full = lambda shape: pl.BlockSpec(shape, lambda b, i: tuple(0 for _ in shape))
    sel = pl.BlockSpec((1, 2, ROW_TILE), lambda b, i: (b * nt + i, 0, 0))
    return pl.pallas_call(
        functools.partial(_ffn_pre_kernel, ctx_len=ctx_len),
        grid=(B, nt),
        in_specs=[tokd, full((1, D))] + _mod_specs(2, ctx_row, (3, 4), D)
                 + [full((N_EXPERTS, D)), full((N_EXPERTS, 1))],
        out_specs=[pl.BlockSpec((ROW_TILE, SUBLANES, D // SUBLANES), lambda b, i: (b * nt + i, 0, 0)), sel, sel],
        out_shape=[jax.ShapeDtypeStruct((B * S, SUBLANES, D // SUBLANES), F32),
                   jax.ShapeDtypeStruct((B * nt, 2, ROW_TILE), I32),
                   jax.ShapeDtypeStruct((B * nt, 2, ROW_TILE), F32)],
        compiler_params=_cparams(("parallel", "parallel")),
        name="ffn_pre",
    )(x, gain.reshape(1, D), mods, mods, mods, mods, w_router.T, router_bias.reshape(N_EXPERTS, 1))


def _moe_dispatch(e, w, n_tok):
    R = MOE_ROWS
    A = 2 * n_tok
    ef = e.reshape(A)
    wf = w.reshape(A)
    onehot = (ef[:, None] == jnp.arange(N_EXPERTS, dtype=I32)[None, :]).astype(I32)
    csum = jnp.cumsum(onehot, axis=0)
    rank = jnp.take_along_axis(csum, ef[:, None], axis=1)[:, 0] - 1
    sizes = csum[-1]
    padded = (sizes + R - 1) // R * R
    pad_ends = jnp.cumsum(padded)
    pad_starts = pad_ends - padded
    dest = pad_starts[ef] + rank
    n_blocks = -(-A // R) + N_EXPERTS
    rows = n_blocks * R
    pair = jnp.arange(A, dtype=I32)
    ridx = jnp.arange(rows, dtype=I32)
    table = jnp.stack([A + ridx % R, jnp.zeros((rows,), I32)], axis=1)
    table = table.at[dest].set(jnp.stack([pair, lax.bitcast_convert_type(wf, I32)], axis=1))
    row_dst = table[:, 0]
    row_w = lax.bitcast_convert_type(table[:, 1], F32)
    row_src = jnp.where(row_dst >= A, 0, jnp.where(row_dst >= n_tok, row_dst - n_tok, row_dst))
    block_e = jnp.minimum(jnp.searchsorted(pad_ends, jnp.arange(n_blocks, dtype=I32) * R, side='right'),
                          N_EXPERTS - 1).astype(I32)
    return row_src.reshape(n_blocks, 1, R), row_dst.reshape(n_blocks, 1, R), row_w.reshape(rows, 1), block_e


def _moe_kernel(be_ref, src_ref, nsrc_ref, dst_ref, rw_ref, h_hbm, wg_ref, wu_ref, wd_ref, y_hbm,
                xbuf, ybuf, sem_in, sem_out):
    del be_ref
    R = MOE_ROWS
    b = pl.program_id(0)
    nb = pl.num_programs(0)
    slot = lax.rem(b, 2)
    other = 1 - slot

    def row_in(idx_ref, r, sl):
        return pltpu.make_async_copy(h_hbm.at[idx_ref[0, 0, r]],
                                     xbuf.at[sl, r // SUBLANES, :, r % SUBLANES, :], sem_in.at[sl])

    def row_out(r, sl):
        return pltpu.make_async_copy(ybuf.at[sl, r // SUBLANES, :, r % SUBLANES, :],
                                     y_hbm.at[dst_ref[0, 0, r]], sem_out.at[sl])

    @pl.when(b == 0)
    def _():
        for r in range(R):
            row_in(src_ref, r, slot).start()

    for r in range(R):
        row_in(src_ref, r, slot).wait()
    for r in range(R):
        row_in(nsrc_ref, r, other).start(priority=r % 2)
    nchunk = xbuf.shape[2]
    lanes = xbuf.shape[4]
    xb = jnp.concatenate([xbuf[slot, :, c, :, :].reshape(R, lanes) for c in range(nchunk)],
                         axis=1).astype(BF16)
    gate = _dot(xb, wg_ref[0])
    up = _dot(xb, wu_ref[0])
    act = (_silu(gate) * up).astype(BF16)
    y = _dot(act, wd_ref[0]) * rw_ref[...]
    for c in range(nchunk):
        ybuf[slot, :, c, :, :] = y[:, c * lanes:(c + 1) * lanes].reshape(R // SUBLANES, SUBLANES, lanes)

    @pl.when(b >= 1)
    def _():
        for r in range(R):
            row_out(r, other).wait()

    for r in range(R):
        row_out(r, slot).start(priority=r % 2)

    @pl.when(b == nb - 1)
    def _():
        for r in range(R):
            row_out(r, slot).wait()
        for r in range(R):
            row_in(nsrc_ref, r, other).wait()


def _moe(h2, row_src, row_dst, row_w, block_e, wg, wu, wd):
    n_tok, ts, tw = h2.shape
    D = ts * tw
    n_blocks = block_e.shape[0]
    R = MOE_ROWS
    F = wg.shape[-1]
    grid_spec = pltpu.PrefetchScalarGridSpec(
        num_scalar_prefetch=1,
        grid=(n_blocks,),
        in_specs=[pl.BlockSpec((1, 1, R), lambda b, be: (b, 0, 0), memory_space=pltpu.SMEM),
                  pl.BlockSpec((1, 1, R), lambda b, be: (jnp.minimum(b + 1, n_blocks - 1), 0, 0),
                               memory_space=pltpu.SMEM),
                  pl.BlockSpec((1, 1, R), lambda b, be: (b, 0, 0), memory_space=pltpu.SMEM),
                  pl.BlockSpec((R, 1), lambda b, be: (b, 0)),
                  pl.BlockSpec(memory_space=pl.ANY),
                  pl.BlockSpec((1, D, F), lambda b, be: (be[b], 0, 0)),
                  pl.BlockSpec((1, D, F), lambda b, be: (be[b], 0, 0)),
                  pl.BlockSpec((1, F, D), lambda b, be: (be[b], 0, 0))],
        out_specs=pl.BlockSpec(memory_space=pl.ANY),
        scratch_shapes=[pltpu.VMEM((2, R // SUBLANES, ts, SUBLANES, tw), F32),
                        pltpu.VMEM((2, R // SUBLANES, ts, SUBLANES, tw), F32),
                        pltpu.SemaphoreType.DMA((2,)), pltpu.SemaphoreType.DMA((2,))],
    )
    return pl.pallas_call(
        _moe_kernel,
        grid_spec=grid_spec,
        out_shape=jax.ShapeDtypeStruct((2 * n_tok + R, ts, tw), F32),
        compiler_params=_cparams(("arbitrary",)),
        name="moe",
    )(block_e, row_src, row_src, row_dst, row_w, h2, wg, wu, wd)


def _ffn_add_kernel(x_ref, m5b_ref, m5c_ref, y0_ref, y1_ref, o_ref, *, ctx_len):
    i = pl.program_id(1)
    n = x_ref.shape[1]
    row = i * n + lax.broadcasted_iota(I32, (n, 1), 0)
    m5 = jnp.where(row < ctx_len, m5c_ref[0], m5b_ref[0])
    o_ref[0] = x_ref[0] + m5 * (_load_token_tiles(y0_ref) + _load_token_tiles(y1_ref))


def _ffn_add(x, mods, ctx_row, y, ctx_len):
    B, S, D = x.shape
    nt = S // ROW_TILE
    tokd = pl.BlockSpec((1, ROW_TILE, D), lambda b, i: (b, i, 0))
    yspec = lambda slot: pl.BlockSpec((ROW_TILE, SUBLANES, D // SUBLANES),
                                      lambda b, i: (slot * B * nt + b * nt + i, 0, 0))
    return pl.pallas_call(
        functools.partial(_ffn_add_kernel, ctx_len=ctx_len),
        grid=(B, nt),
        in_specs=[tokd] + _mod_specs(2, ctx_row, (5,), D) + [yspec(0), yspec(1)],
        out_specs=tokd,
        out_shape=jax.ShapeDtypeStruct((B, S, D), F32),
        input_output_aliases={0: 0},
        compiler_params=_cparams(("parallel", "parallel")),
        name="ffn_add",
    )(x, mods, mods, y, y)


def _final_kernel(x_ref, m5_ref, y0_ref, y1_ref, g_ref, o_ref):
    x = x_ref[0] + m5_ref[0] * (_load_token_tiles(y0_ref) + _load_token_tiles(y1_ref))
    o_ref[0] = x * lax.rsqrt(jnp.mean(x * x, axis=-1, keepdims=True) + EPS) * g_ref[...]


def _final(x, mods, y, gain, ctx_len):
    B, S, D = x.shape
    nt = S // ROW_TILE
    skip = ctx_len // ROW_TILE
    tok_in = pl.BlockSpec((1, ROW_TILE, D), lambda b, i: (b, i + skip, 0))
    yspec = lambda slot: pl.BlockSpec((ROW_TILE, SUBLANES, D // SUBLANES),
                                      lambda b, i: (slot * B * nt + b * nt + i + skip, 0, 0))
    return pl.pallas_call(
        _final_kernel,
        grid=(B, nt - skip),
        in_specs=[tok_in, pl.BlockSpec((1, 1, D), lambda b, i: (b, 0, 5)), yspec(0), yspec(1),
                  pl.BlockSpec((1, D), lambda b, i: (0, 0))],
        out_specs=pl.BlockSpec((1, ROW_TILE, D), lambda b, i: (b, i, 0)),
        out_shape=jax.ShapeDtypeStruct((B, S - ctx_len, D), F32),
        compiler_params=_cparams(("parallel", "parallel")),
        name="final_norm",
    )(x, mods, y, y, gain.reshape(1, D))


def _pad_w_in(w):
    W = BRANCH_WIDTH
    a = w[:, :A_COLS]
    qkvz = w[:, A_COLS:A_COLS + 4 * W]
    ab = w[:, A_COLS + 4 * W:A_COLS + 4 * W + 4 * B_HEADS]
    rest = w[:, A_COLS + 4 * W + 4 * B_HEADS:]
    ab = jnp.pad(ab, ((0, 0), (0, LANES - 4 * B_HEADS)))
    out = jnp.concatenate([a, ab, qkvz, rest], axis=1).astype(BF16)
    assert out.shape[1] == N_IN_PAD
    return out


def kernel(x, c, ctx, c_ctx, w_ada, b_ada, norm_mix, norm_ffn, norm_final, w_in, rwkv_mu, rwkv_w0, rwkv_w2, rwkv_a0, rwkv_a2, rwkv_g2, rwkv_kk, rwkv_ka, rwkv_rk, rwkv_ln_g, rwkv_ln_b, gdn_conv, gdn_A_log, gdn_dt_bias, gdn_norm, conf_dw, conf_dw_b, conf_ln_g, conf_ln_b, w_branch, w_out, w_router, router_bias, w_e_gate, w_e_up, w_e_down):
    B, T, D = x.shape
    ctx_len = ctx.shape[1]
    depth = w_in.shape[0]
    assert ctx_len == ROW_TILE and T % ROW_TILE == 0 and T % GRID_W == 0
    S = ctx_len + T
    n_tok = B * S

    xs = jnp.concatenate([ctx, x], axis=1)
    rows = -(-(B + 1) // SUBLANES) * SUBLANES
    svec = jnp.concatenate([c, c_ctx[None, :], jnp.zeros((rows - B - 1, D), F32)], axis=0)
    mods_all = _ada(svec, w_ada, b_ada)
    ctx_row = B

    out = None
    for l in range(depth):
        mods = mods_all[l].reshape(rows, 1, 6 * D)
        P = _inproj(xs, norm_mix[l], mods, ctx_row, _pad_w_in(w_in[l]), ctx_len)
        prep = _rwkv_prep(P, rwkv_mu[l], rwkv_w0[l], rwkv_w2[l], rwkv_a0[l], rwkv_a2[l], rwkv_g2[l],
                          rwkv_kk[l], rwkv_ka[l], rwkv_rk[l])
        bonus, gate = prep[11], prep[12]
        y0, y1 = _rwkv_scan(prep[:11], ctx_len)
        q, k, v, gcol, grow = _gdn_prep(P, gdn_conv[l], gdn_A_log[l], gdn_dt_bias[l])
        o0, o1 = _gdn_scan(q, k, v, gcol, grow, ctx_len)
        ycv = _conformer_conv(P, conf_dw[l], ctx_len)
        last = l == depth - 1
        xs = _merge(xs, mods, ctx_row, y0, y1, bonus, gate, o0, o1, P, ycv,
                    rwkv_ln_g[l], rwkv_ln_b[l], gdn_norm[l], conf_dw_b[l], conf_ln_g[l], conf_ln_b[l],
                    w_branch[l].astype(BF16), w_out[l].astype(BF16), ctx_len, latent_only=last)
        ctx_rows = 0 if last else ctx_len
        n_rows = xs.shape[0] * xs.shape[1]
        h2, e, w = _ffn_pre(xs, norm_ffn[l], mods, ctx_row, w_router, router_bias, ctx_rows)
        e = e.transpose(1, 0, 2).reshape(2, n_rows)
        w = w.transpose(1, 0, 2).reshape(2, n_rows)
        row_src, row_dst, row_w, block_e = _moe_dispatch(e, w, n_rows)
        y = _moe(h2, row_src, row_dst, row_w, block_e,
                 w_e_gate[l].astype(BF16), w_e_up[l].astype(BF16), w_e_down[l].astype(BF16))
        if last:
            out = _final(xs, mods, y, norm_final, ctx_rows)
        else:
            xs = _ffn_add(xs, mods, ctx_row, y, ctx_len)
    return out
```

```python
import functools
import math

import jax
import jax.numpy as jnp
from jax import lax
from jax.experimental import pallas as pl
from jax.experimental.pallas import tpu as pltpu

F32 = jnp.float32
BF16 = jnp.bfloat16
I32 = jnp.int32
HIGHEST = lax.Precision.HIGHEST

EPS = 1e-6
LN_X_EPS = 64e-5
GRID_W = 64
BRANCH_WIDTH = 512
A_HEADS = 8
A_HEAD_DIM = 64
B_HEADS = 4
B_HEAD_DIM = 128
DECAY_LORA = 64
SHORT_CONV = 5
DW_CONV = 31
N_EXPERTS = 16
N_GROUPS = 4
EXP_PER_GROUP = 4
D_EXPERT = 512

ROW_TILE = 256
CHUNK = 64
RWKV_SCAN_CHUNKS = 4
MOE_ROWS = 128
LANES = 128
SUBLANES = 8
HALO_ROWS = 16
VMEM_LIMIT = 56 * 1024 * 1024

A_COLS = 1920
COL_A = 0
COL_AB = 1920
COL_Q = 2048
COL_Z = 3584
COL_C = 4096
COL_G = 5120
N_IN_PAD = 8192
LORA_W = 3 * LANES


def _cparams(sem):
    return pltpu.CompilerParams(dimension_semantics=sem, vmem_limit_bytes=VMEM_LIMIT)


def _sigmoid(x):
    return 0.5 * jnp.tanh(0.5 * x) + 0.5


def _silu(x):
    return x * _sigmoid(x)


def _softplus(x):
    return jnp.maximum(x, 0.0) + jnp.log(1.0 + jnp.exp(-jnp.abs(x)))


def _dot(a, b):
    return jnp.dot(a, b, preferred_element_type=F32)


def _dot_nt(a, b):
    return lax.dot_general(a, b, (((1,), (1,)), ((), ())), preferred_element_type=F32)


def _bf16_parts(x, n):
    parts = []
    r = x
    for _ in range(n):
        p = r.astype(BF16)
        parts.append(p)
        r = r - p.astype(F32)
    return parts


def _dot_sel_left(m, x, n=3):
    acc = None
    for p in _bf16_parts(x, n):
        t = _dot(m, p)
        acc = t if acc is None else acc + t
    return acc


def _dot_sel_right(x, m, n=2):
    acc = None
    for p in _bf16_parts(x, n):
        t = _dot(p, m)
        acc = t if acc is None else acc + t
    return acc


def _group_ones(n, group):
    r = lax.broadcasted_iota(I32, (n, n), 0) // group
    c = lax.broadcasted_iota(I32, (n, n), 1) // group
    return jnp.where(r == c, 1.0, 0.0).astype(BF16)


def _chunk_tri(n, lower):
    r = lax.broadcasted_iota(I32, (n, n), 0)
    c = lax.broadcasted_iota(I32, (n, n), 1)
    same = (r // CHUNK) == (c // CHUNK)
    keep = (c <= r) if lower else (c >= r)
    return jnp.where(same & keep, 1.0, 0.0).astype(BF16)


def _shift_down(x, prev_rows, s):
    n = x.shape[0]
    hr = prev_rows.shape[0]
    out = pltpu.roll(x, s, axis=0)
    row = lax.broadcasted_iota(I32, (n, 1), 0)
    for q in range(s):
        out = jnp.where(row == q, prev_rows[hr - s + q:hr - s + q + 1, :], out)
    return out


def _shift_up(x, next_rows, s):
    n = x.shape[0]
    out = pltpu.roll(x, n - s, axis=0)
    row = lax.broadcasted_iota(I32, (n, 1), 0)
    for q in range(s):
        out = jnp.where(row == n - s + q, next_rows[q:q + 1, :], out)
    return out


def _row_shifts(x_bf16, prev_rows, next_rows, offsets):
    n = x_bf16.shape[0]
    hr = prev_rows.shape[0]
    r = lax.broadcasted_iota(I32, (n, n), 0)
    c = lax.broadcasted_iota(I32, (n, n), 1)
    sel = jnp.concatenate([jnp.where(c == r + off, 1.0, 0.0).astype(BF16) for off in offsets], axis=0)
    sh = _dot(sel, x_bf16)
    row8 = lax.broadcasted_iota(I32, (SUBLANES, 1), 0)
    outs = []
    for q, off in enumerate(offsets):
        o = sh[q * n:(q + 1) * n]
        s = abs(off)
        assert 0 < s <= SUBLANES
        if off < 0:
            edge = o[:SUBLANES]
            for t in range(s):
                edge = jnp.where(row8 == t, prev_rows[hr - s + t:hr - s + t + 1, :], edge)
            o = jnp.concatenate([edge, o[SUBLANES:]], axis=0)
        else:
            edge = o[n - SUBLANES:]
            for t in range(s):
                edge = jnp.where(row8 == SUBLANES - s + t, next_rows[t:t + 1, :], edge)
            o = jnp.concatenate([o[:n - SUBLANES], edge], axis=0)
        outs.append(o)
    return outs


def _halo_rows(p_prev_ref, p_next_ref, i, n_tiles):
    prev_rows = p_prev_ref[0].astype(F32)
    next_rows = p_next_ref[0].astype(F32)
    prev_rows = jnp.where(i <= 1, 0.0, prev_rows)
    next_rows = jnp.where((i == 0) | (i == n_tiles - 1), 0.0, next_rows)
    return prev_rows, next_rows


def _halo_specs(width, col, tiles_per_batch):
    assert col % width == 0
    col_block = col // width
    sub_per_tile = ROW_TILE // HALO_ROWS
    last = tiles_per_batch * sub_per_tile - 1
    main = pl.BlockSpec((1, ROW_TILE, width), lambda b, i: (b, i, col_block))
    prev = pl.BlockSpec((1, HALO_ROWS, width),
                        lambda b, i: (b, jnp.maximum(i * sub_per_tile - 1, 0), col_block))
    nxt = pl.BlockSpec((1, HALO_ROWS, width),
                       lambda b, i: (b, jnp.minimum((i + 1) * sub_per_tile, last), col_block))
    return [main, prev, nxt]


def _ada_kernel(s_ref, w_ref, b_ref, o_ref):
    s = s_ref[...]
    o_ref[0] = jnp.dot(_silu(s), w_ref[0], preferred_element_type=F32, precision=HIGHEST) + b_ref[0]


def _ada(svec, w_ada, b_ada):
    L, D, N = w_ada.shape
    rows = svec.shape[0]
    tn = 1536
    return pl.pallas_call(
        _ada_kernel,
        grid=(L, N // tn),
        in_specs=[pl.BlockSpec((rows, D), lambda l, j: (0, 0)),
                  pl.BlockSpec((1, D, tn), lambda l, j: (l, 0, j)),
                  pl.BlockSpec((1, 1, tn), lambda l, j: (l, 0, j))],
        out_specs=pl.BlockSpec((1, rows, tn), lambda l, j: (l, 0, j)),
        out_shape=jax.ShapeDtypeStruct((L, rows, N), F32),
        compiler_params=_cparams(("parallel", "parallel")),
        name="ada",
    )(svec, w_ada, b_ada.reshape(L, 1, N))


def _store_token_tiles(ref, x):
    w = x.shape[1] // SUBLANES
    for s in range(SUBLANES):
        ref[:, s, :] = x[:, s * w:(s + 1) * w]


def _load_token_tiles(ref):
    return jnp.concatenate([ref[:, s, :] for s in range(SUBLANES)], axis=1)


def _modulated(x, gain, mods, row0, ctx_len):
    sh_b, sc_b, sh_c, sc_c = mods
    y = x * lax.rsqrt(jnp.mean(x * x, axis=-1, keepdims=True) + EPS) * gain
    row = row0 + lax.broadcasted_iota(I32, (x.shape[0], 1), 0)
    isc = row < ctx_len
    shift = jnp.where(isc, sh_c, sh_b)
    scale = jnp.where(isc, sc_c, sc_b)
    return y * (1.0 + scale) + shift


def _mod_specs(n_grid, ctx_row, chunks, d):
    def spec(row_of, c):
        if n_grid == 2:
            return pl.BlockSpec((1, 1, d), lambda b, i: (row_of(b), 0, c))
        return pl.BlockSpec((1, 1, d), lambda b, i, j: (row_of(b), 0, c))
    return ([spec(lambda b: b, c) for c in chunks] + [spec(lambda b: ctx_row, c) for c in chunks])


def _inproj_kernel(x_ref, g_ref, shb_ref, scb_ref, shc_ref, scc_ref, w_ref, o_ref, h_scr, *, tm, ctx_len):
    i = pl.program_id(1)
    j = pl.program_id(2)

    @pl.when(j == 0)
    def _():
        mods = (shb_ref[0], scb_ref[0], shc_ref[0], scc_ref[0])
        h_scr[...] = _modulated(x_ref[0], g_ref[...], mods, i * tm, ctx_len).astype(BF16)

    o_ref[0] = _dot(h_scr[...], w_ref[...]).astype(o_ref.dtype)


def _inproj(x, gain, mods, ctx_row, wp, ctx_len):
    B, S, D = x.shape
    N = wp.shape[1]
    tm = S // 2
    tn = 1024
    return pl.pallas_call(
        functools.partial(_inproj_kernel, tm=tm, ctx_len=ctx_len),
        grid=(B, S // tm, N // tn),
        in_specs=[pl.BlockSpec((1, tm, D), lambda b, i, j: (b, i, 0)),
                  pl.BlockSpec((1, D), lambda b, i, j: (0, 0))]
                 + _mod_specs(3, ctx_row, (0, 1), D)
                 + [pl.BlockSpec((D, tn), lambda b, i, j: (0, j))],
        out_specs=pl.BlockSpec((1, tm, tn), lambda b, i, j: (b, i, j)),
        out_shape=jax.ShapeDtypeStruct((B, S, N), BF16),
        scratch_shapes=[pltpu.VMEM((tm, D), BF16)],
        compiler_params=_cparams(("parallel", "parallel", "arbitrary")),
        name="inproj",
    )(x, gain.reshape(1, D), mods, mods, mods, mods, wp)


def _rwkv_prep_kernel(r_ref, rp_ref, rn_ref, k_ref, kp_ref_, kn_ref, vv_ref, vp_ref, vn_ref,
                      x_ref, xp_ref, xn_ref,
                      mu_ref, w0_ref, w2_ref, a0_ref, a2_ref, g2_ref, kk_ref, ka_ref, rk_ref,
                      rt0_ref, kp0_ref, bt0_ref, kt0_ref, pc0_ref,
                      rt1_ref, kp1_ref, bt1_ref, kt1_ref, pc1_ref,
                      v_ref, bonus_ref, g_ref, *, n_tiles):
    i = pl.program_id(1)
    W = BRANCH_WIDTH

    def shifted(main, prv, nxt, col, width):
        p = main[0].astype(F32)
        prev_rows, next_rows = _halo_rows(prv, nxt, i, n_tiles)
        mu = mu_ref[:, col:col + width]
        return (p + mu[0:1] * (_shift_down(p, prev_rows, 1) - p)
                + mu[1:2] * (_shift_up(p, next_rows, 1) - p))

    r = shifted(r_ref, rp_ref, rn_ref, 0, W)
    k = shifted(k_ref, kp_ref_, kn_ref, W, W)
    v = shifted(vv_ref, vp_ref, vn_ref, 2 * W, W)
    lora = shifted(x_ref, xp_ref, xn_ref, 3 * W, LORA_W)
    xw = jnp.tanh(lora[:, 0:LANES])
    xa = lora[:, LANES:2 * LANES]
    xg = _sigmoid(lora[:, 2 * LANES:3 * LANES])
    lane_hi = lax.broadcasted_iota(I32, xw.shape, 1) >= DECAY_LORA

    g_ref[0] = _dot(xg.astype(BF16), g2_ref[...].astype(BF16)).astype(g_ref.dtype)
    v_ref[0] = v.astype(BF16)

    gsum = _group_ones(W, A_HEAD_DIM)
    kx = k * kk_ref[...]
    kk = kx * lax.rsqrt(_dot_sel_right(kx * kx, gsum) + EPS)
    bonus_ref[0] = (_dot_sel_right(r * k * rk_ref[...], gsum) * v).astype(bonus_ref.dtype)

    w2 = w2_ref[...].astype(BF16)
    a2 = a2_ref[...].astype(BF16)
    n = r.shape[0]
    outs = ((rt0_ref, kp0_ref, bt0_ref, kt0_ref, pc0_ref), (rt1_ref, kp1_ref, bt1_ref, kt1_ref, pc1_ref))
    for d in range(2):
        keep = lane_hi if d == 1 else jnp.logical_not(lane_hi)
        lw = w0_ref[d:d + 1, :] + _dot(jnp.where(keep, xw, 0.0).astype(BF16), w2)
        ew = _sigmoid(lw) * math.exp(-0.5)
        a = _sigmoid(a0_ref[d:d + 1, :] + _dot(jnp.where(keep, xa, 0.0).astype(BF16), a2))
        keff = k * (1.0 + (a - 1.0) * ka_ref[...])
        cs = _dot_sel_left(_chunk_tri(n, lower=(d == 0)), ew, n=2)
        p_incl = jnp.exp(-cs)
        p_excl = jnp.exp(ew - cs)
        p_inv = jnp.exp(cs)
        rt_ref, kp_ref, bt_ref, kt_ref, pc_ref = outs[d]
        rt_ref[0] = (r * p_incl).astype(BF16)
        kp_ref[0] = (kk * p_excl).astype(BF16)
        bt_ref[0] = (kk * a * p_inv).astype(BF16)
        kt_ref[0] = (keff * p_inv).astype(BF16)
        for c in range(n // CHUNK):
            row = c * CHUNK + (CHUNK - 1 if d == 0 else 0)
            pc_ref[0, c] = p_incl[row:row + 1, :]


def _rwkv_prep(P, mu, w0, w2, a0, a2, g2, k_k, k_a, r_k):
    B, S, _ = P.shape
    W = BRANCH_WIDTH
    nt = S // ROW_TILE
    cpt = ROW_TILE // CHUNK
    halo = (_halo_specs(W, COL_A, nt) + _halo_specs(W, COL_A + W, nt) + _halo_specs(W, COL_A + 2 * W, nt)
            + _halo_specs(LORA_W, COL_A + 3 * W, nt))
    full = lambda shape: pl.BlockSpec(shape, lambda b, i: tuple(0 for _ in shape))
    tok = pl.BlockSpec((1, ROW_TILE, W), lambda b, i: (b, i, 0))
    pcs = pl.BlockSpec((1, cpt, 1, W), lambda b, i: (b, i, 0, 0))
    bf = jax.ShapeDtypeStruct((B, S, W), BF16)
    f32 = jax.ShapeDtypeStruct((B, S, W), F32)
    pcshape = jax.ShapeDtypeStruct((B, S // CHUNK, 1, W), F32)
    return pl.pallas_call(
        functools.partial(_rwkv_prep_kernel, n_tiles=nt),
        grid=(B, nt),
        in_specs=halo + [full((2, A_COLS)), full((2, W)), full((2 * DECAY_LORA, W)), full((2, W)),
                         full((2 * DECAY_LORA, W)), full((LANES, W)), full((1, W)), full((1, W)),
                         full((1, W))],
        out_specs=[tok, tok, tok, tok, pcs, tok, tok, tok, tok, pcs, tok, tok, tok],
        out_shape=[bf, bf, bf, bf, pcshape, bf, bf, bf, bf, pcshape, bf, bf, bf],
        compiler_params=_cparams(("parallel", "parallel")),
        name="rwkv_prep",
    )(*([P] * 12), mu, w0, w2.reshape(2 * DECAY_LORA, W), a0, a2.reshape(2 * DECAY_LORA, W), g2,
      k_k.reshape(1, W), k_a.reshape(1, W), r_k.reshape(1, W))


def _unit_lower_inverse(a_neg, steps):
    n = a_neg.shape[0]
    eye = jnp.where(lax.broadcasted_iota(I32, (n, n), 0) == lax.broadcasted_iota(I32, (n, n), 1), 1.0, 0.0)
    t = eye + a_neg
    xp = a_neg.astype(BF16)
    for _ in range(steps - 1):
        x2 = _dot(xp, xp)
        xp = x2.astype(BF16)
        t = t + _dot(t.astype(BF16), xp)
    return t


def _run_chains(chains):
    while chains:
        alive = []
        for g in chains:
            try:
                next(g)
                alive.append(g)
            except StopIteration:
                pass
        chains = alive


def _stack_heads(x, lane_hi):
    z = jnp.zeros_like(x)
    return jnp.concatenate([jnp.where(lane_hi, z, x), jnp.where(lane_hi, x, z)], axis=0)


def _rwkv_scan_kernel(rt0, kp0, bt0, kt0, v0, pc0, rt1, kp1, bt1, kt1, v1, pc1, y0_ref, y1_ref, st_ref):
    s = pl.program_id(1)

    @pl.when(s == 0)
    def _():
        st_ref[...] = jnp.zeros_like(st_ref)

    C = CHUNK
    n2 = 2 * C
    ri = lax.broadcasted_iota(I32, (n2, n2), 0)
    ci = lax.broadcasted_iota(I32, (n2, n2), 1)
    same = (ri // C) == (ci // C)
    tr = ri % C
    ts = ci % C
    lane_hi = lax.broadcasted_iota(I32, (C, n2), 1) >= C
    eye = jnp.where(ri == ci, 1.0, 0.0)
    dirs = ((rt0, kp0, bt0, kt0, v0, pc0, y0_ref), (rt1, kp1, bt1, kt1, v1, pc1, y1_ref))
    nch = rt0.shape[1] // C
    ready = {}

    def intra(d, j, cc):
        rt, kp, bt, kt, vv, pc, _ = dirs[d]
        strict = same & ((ts < tr) if d == 0 else (ts > tr))
        incl = same & ((ts <= tr) if d == 0 else (ts >= tr))
        sl = slice(n2 * j, n2 * (j + 1))
        rows = slice(cc * C, (cc + 1) * C)
        kap = kp[0, rows, sl]
        r_ = rt[0, rows, sl]
        b_ = bt[0, rows, sl]
        k_ = kt[0, rows, sl]
        v_st = _stack_heads(vv[0, rows, sl], lane_hi)
        aa = _dot_nt(jnp.concatenate([_stack_heads(kap, lane_hi), _stack_heads(r_, lane_hi)], axis=0),
                     jnp.concatenate([b_, b_, k_, k_], axis=0))
        yield
        x = jnp.where(strict, -aa[:n2, :n2], 0.0)
        a_ak = jnp.where(strict, aa[:n2, n2:], 0.0).astype(BF16)
        a_rb = jnp.where(incl, aa[n2:, :n2], 0.0).astype(BF16)
        a_rk = jnp.where(incl, aa[n2:, n2:], 0.0).astype(BF16)
        t = eye + x
        xp = x.astype(BF16)
        b_t = _stack_heads(b_, lane_hi).astype(F32).T.astype(BF16)
        k_t = _stack_heads(k_, lane_hi).astype(F32).T.astype(BF16)
        sv = _dot(jnp.concatenate([a_ak, a_rk, k_t], axis=0), v_st)
        akv = sv[:n2]
        arkv = sv[n2:2 * n2]
        kv0 = sv[2 * n2:]
        x2 = _dot(xp, xp)
        yield
        for _ in range(4):
            xp = x2.astype(BF16)
            tx = _dot(jnp.concatenate([t.astype(BF16), xp], axis=0), xp)
            yield
            t = t + tx[:n2]
            x2 = tx[n2:]
        t = t + _dot(t.astype(BF16), x2.astype(BF16))
        yield
        pcm = jnp.broadcast_to(pc[0, cc][:, sl], (n2, n2)).T
        ready[(d, j, cc)] = (jnp.concatenate([kap, r_], axis=0), t.astype(BF16), akv, arkv, kv0,
                             jnp.concatenate([a_rb, b_t], axis=0), pcm)

    def state(d, j):
        y_ref = dirs[d][6]
        sl = slice(n2 * j, n2 * (j + 1))
        h = st_ref[d, j]
        for cc in (range(nch) if d == 0 else range(nch - 1, -1, -1)):
            while (d, j, cc) not in ready:
                yield
            kap_r, t_b, akv, arkv, kv0, arb_bt, pcm = ready.pop((d, j, cc))
            krh = _dot(kap_r, h.astype(BF16))
            yield
            rhs = akv + _stack_heads(krh[:C], lane_hi)
            u_st = -_dot(t_b, rhs.astype(BF16))
            yield
            yd = _dot(arb_bt, u_st.astype(BF16))
            yield
            y_st = yd[:n2] + arkv
            y_ref[0, cc * C:(cc + 1) * C, sl] = (y_st[:C] + y_st[C:] + krh[C:]).astype(y_ref.dtype)
            h = pcm * (h + kv0 + yd[n2:])
        st_ref[d, j] = h

    units = [(d, j) for d in range(2) for j in range(BRANCH_WIDTH // n2)]
    chains = []
    for pos in range(nch):
        for d, j in units:
            chains.append(intra(d, j, pos if d == 0 else nch - 1 - pos))
    _run_chains(chains + [state(d, j) for d, j in units])


def _scan_tile_maps(nt_ctx, nt):
    def t0(s):
        return s

    def t1(s):
        return jnp.where(s < nt_ctx, nt_ctx - 1 - s, nt + nt_ctx - 1 - s)

    return t0, t1


def _rwkv_scan(prep, ctx_len):
    rt0, kp0, bt0, kt0, pc0, rt1, kp1, bt1, kt1, pc1, v = prep
    B, S, W = v.shape
    rows = RWKV_SCAN_CHUNKS * CHUNK
    c0, c1 = _scan_tile_maps(ctx_len // rows, S // rows)
    tok = lambda cm: pl.BlockSpec((1, rows, W), lambda b, s: (b, cm(s), 0))
    pcs = lambda cm: pl.BlockSpec((1, RWKV_SCAN_CHUNKS, 1, W), lambda b, s: (b, cm(s), 0, 0))
    y = jax.ShapeDtypeStruct((B, S, W), BF16)
    return pl.pallas_call(
        _rwkv_scan_kernel,
        grid=(B, S // rows),
        in_specs=[tok(c0), tok(c0), tok(c0), tok(c0), tok(c0), pcs(c0),
                  tok(c1), tok(c1), tok(c1), tok(c1), tok(c1), pcs(c1)],
        out_specs=[tok(c0), tok(c1)],
        out_shape=[y, y],
        scratch_shapes=[pltpu.VMEM((2, W // (2 * CHUNK), 2 * CHUNK, 2 * CHUNK), F32)],
        compiler_params=_cparams(("parallel", "arbitrary")),
        name="rwkv_scan",
    )(rt0, kp0, bt0, kt0, v, pc0, rt1, kp1, bt1, kt1, v, pc1)


def _gdn_prep_kernel(qi_ref, qp_ref, qn_ref, ki_ref, kp_ref, kn_ref, vi_ref, vp_ref, vn_ref,
                     ab_ref, cw_ref, al_ref, dt_ref,
                     q_ref, k_ref, v_ref, gcol_ref, grow_ref, *, n_tiles):
    i = pl.program_id(1)
    W = BRANCH_WIDTH
    half = SHORT_CONV // 2

    def conv_silu(main, prv, nxt, col):
        x = main[0].astype(F32)
        prev_rows, next_rows = _halo_rows(prv, nxt, i, n_tiles)
        cw = cw_ref[:, col:col + W]
        offsets = tuple(o for o in range(-half, half + 1) if o != 0)
        acc = cw[half:half + 1] * x
        for off, xs in zip(offsets, _row_shifts(main[0], prev_rows, next_rows, offsets)):
            acc = acc + cw[half + off:half + off + 1] * xs
        return _silu(acc)

    yq = conv_silu(qi_ref, qp_ref, qn_ref, 0)
    yk = conv_silu(ki_ref, kp_ref, kn_ref, W)
    v_ref[0] = conv_silu(vi_ref, vp_ref, vn_ref, 2 * W).astype(v_ref.dtype)
    for hd in range(B_HEADS):
        sl = slice(hd * B_HEAD_DIM, (hd + 1) * B_HEAD_DIM)
        q = yq[:, sl]
        q_ref[0, :, sl] = (q * lax.rsqrt(jnp.sum(q * q, axis=-1, keepdims=True) + EPS)
                           * (B_HEAD_DIM ** -0.5)).astype(q_ref.dtype)
        k = yk[:, sl]
        k_ref[0, :, sl] = (k * lax.rsqrt(jnp.sum(k * k, axis=-1, keepdims=True) + EPS)).astype(k_ref.dtype)

    ab = ab_ref[0].astype(F32)
    n = ab.shape[0]
    lane = lax.broadcasted_iota(I32, ab.shape, 1)
    g = -jnp.exp(al_ref[...]) * _softplus(ab + dt_ref[...])
    g = jnp.where(lane < 2 * B_HEADS, g, 0.0)
    gc_f = _dot_sel_left(_chunk_tri(n, lower=True), g)
    gc_b = _dot_sel_left(_chunk_tri(n, lower=False), g)
    gc = jnp.where(lane < B_HEADS, gc_f, gc_b)
    beta = _sigmoid(ab)
    out = jnp.where(lane < 2 * B_HEADS, gc, jnp.where(lane < 4 * B_HEADS, beta, 0.0))
    gcol_ref[0] = out
    grow_ref[0] = out.T[:4 * B_HEADS, :]


def _gdn_prep(P, conv_w, A_log, dt_bias):
    B, S, _ = P.shape
    W = BRANCH_WIDTH
    nt = S // ROW_TILE
    halo = _halo_specs(W, COL_Q, nt) + _halo_specs(W, COL_Q + W, nt) + _halo_specs(W, COL_Q + 2 * W, nt)
    pad = jnp.zeros((1, LANES - 2 * B_HEADS), F32)
    al = jnp.concatenate([A_log.reshape(1, 2 * B_HEADS), pad], axis=1)
    dt = jnp.concatenate([dt_bias.reshape(1, 2 * B_HEADS), pad], axis=1)
    full = lambda shape: pl.BlockSpec(shape, lambda b, i: tuple(0 for _ in shape))
    tok = pl.BlockSpec((1, ROW_TILE, W), lambda b, i: (b, i, 0))
    bf16 = jax.ShapeDtypeStruct((B, S, W), BF16)
    return pl.pallas_call(
        functools.partial(_gdn_prep_kernel, n_tiles=nt),
        grid=(B, nt),
        in_specs=halo + [pl.BlockSpec((1, ROW_TILE, LANES), lambda b, i: (b, i, COL_AB // LANES)),
                         full((SHORT_CONV, 3 * W)), full((1, LANES)), full((1, LANES))],
        out_specs=[tok, tok, tok,
                   pl.BlockSpec((1, ROW_TILE, LANES), lambda b, i: (b, i, 0)),
                   pl.BlockSpec((1, 4 * B_HEADS, ROW_TILE), lambda b, i: (b, 0, i))],
        out_shape=[bf16, bf16, bf16, jax.ShapeDtypeStruct((B, S, LANES), F32),
                   jax.ShapeDtypeStruct((B, 4 * B_HEADS, S), F32)],
        compiler_params=_cparams(("parallel", "parallel")),
        name="gdn_prep",
    )(*([P] * 10), conv_w, al, dt)


def _gdn_scan_kernel(q0, k0, v0, gc0, gr0, q1, k1, v1, gc1, gr1, o0_ref, o1_ref, st_ref):
    s = pl.program_id(1)

    @pl.when(s == 0)
    def _():
        st_ref[...] = jnp.zeros_like(st_ref)

    C = CHUNK
    nch = q0.shape[1] // C
    ri = lax.broadcasted_iota(I32, (C, C), 0)
    ci = lax.broadcasted_iota(I32, (C, C), 1)
    dirs = ((q0, k0, v0, gc0, gr0, o0_ref), (q1, k1, v1, gc1, gr1, o1_ref))
    eye = jnp.where(ri == ci, 1.0, 0.0)
    ready = {}

    def intra(d, hd, cc):
        qr, kr, vr, gcr, grr, _ = dirs[d]
        incl = (ci <= ri) if d == 0 else (ci >= ri)
        strict = (ci < ri) if d == 0 else (ci > ri)
        last = C - 1 if d == 0 else 0
        rows = slice(cc * C, (cc + 1) * C)
        gcol = gcr[0, rows, :]
        grow = grr[0, cc]
        sl = slice(hd * B_HEAD_DIM, (hd + 1) * B_HEAD_DIM)
        idx = d * B_HEADS + hd
        q = qr[0, rows, sl].astype(F32)
        k = kr[0, rows, sl].astype(F32)
        v = vr[0, rows, sl].astype(F32)
        gc_c = gcol[:, idx:idx + 1]
        gc_r = grow[idx:idx + 1, :]
        beta = gcol[:, 2 * B_HEADS + idx:2 * B_HEADS + idx + 1]
        gc_last = gc_r[:, last:last + 1]
        diff = gc_c - gc_r
        decay_incl = jnp.where(incl, jnp.exp(jnp.where(incl, diff, 0.0)), 0.0)
        decay_strict = jnp.where(strict, decay_incl, 0.0)
        kb = k * beta
        e_gc = jnp.exp(gc_c)
        qk = _dot_nt(jnp.concatenate([kb, q], axis=0).astype(BF16), k.astype(BF16))
        yield
        x = -(qk[:C] * decay_strict)
        attn = (qk[C:] * decay_incl).astype(BF16)
        rhs = jnp.concatenate([v * beta, kb * e_gc], axis=1).astype(BF16)
        q_dec = (q * e_gc).astype(BF16)
        k_dec_t = (k * jnp.exp(gc_last - gc_c)).T.astype(BF16)
        g_last = jnp.exp(gc_last)
        t = eye + x
        xp = x.astype(BF16)
        x2 = _dot(xp, xp)
        yield
        for _ in range(4):
            xp = x2.astype(BF16)
            tx = _dot(jnp.concatenate([t.astype(BF16), xp], axis=0), xp)
            yield
            t = t + tx[:C]
            x2 = tx[C:]
        t = t + _dot(t.astype(BF16), x2.astype(BF16))
        yield
        sol = _dot(t.astype(BF16), rhs)
        yield
        ready[(d, hd, cc)] = (sol[:, :B_HEAD_DIM], sol[:, B_HEAD_DIM:].astype(BF16), q_dec, attn, k_dec_t,
                              g_last)

    def state(d, hd):
        o_ref = dirs[d][5]
        sl = slice(hd * B_HEAD_DIM, (hd + 1) * B_HEAD_DIM)
        st = st_ref[d, hd]
        for cc in (range(nch) if d == 0 else range(nch - 1, -1, -1)):
            while (d, hd, cc) not in ready:
                yield
            u, w, q_dec, attn, k_dec_t, g_last = ready.pop((d, hd, cc))
            ws_qs = _dot(jnp.concatenate([w, q_dec], axis=0), st.astype(BF16))
            yield
            vn_b = (u - ws_qs[:C]).astype(BF16)
            od = _dot(jnp.concatenate([attn, k_dec_t], axis=0), vn_b)
            yield
            o_ref[0, cc * C:(cc + 1) * C, sl] = (ws_qs[C:] + od[:C]).astype(o_ref.dtype)
            st = st * g_last + od[C:]
        st_ref[d, hd] = st

    units = [(d, hd) for d in range(2) for hd in range(B_HEADS)]
    chains = []
    for pos in range(nch):
        for d, hd in units:
            chains.append(intra(d, hd, pos if d == 0 else nch - 1 - pos))
    _run_chains(chains + [state(d, hd) for d, hd in units])


def _gdn_scan(q, k, v, gcol, grow, ctx_len):
    B, S, W = q.shape
    nc = S // CHUNK
    nt = S // ROW_TILE
    cpt = ROW_TILE // CHUNK
    c0, c1 = _scan_tile_maps(ctx_len // ROW_TILE, nt)
    tok = lambda cm: pl.BlockSpec((1, ROW_TILE, W), lambda b, s: (b, cm(s), 0))
    gcs = lambda cm: pl.BlockSpec((1, ROW_TILE, LANES), lambda b, s: (b, cm(s), 0))
    grs = lambda cm: pl.BlockSpec((1, cpt, 4 * B_HEADS, CHUNK), lambda b, s: (b, cm(s), 0, 0))
    grow_c = grow.reshape(B, 4 * B_HEADS, nc, CHUNK).transpose(0, 2, 1, 3)
    o = jax.ShapeDtypeStruct((B, S, W), BF16)
    return pl.pallas_call(
        _gdn_scan_kernel,
        grid=(B, nt),
        in_specs=[tok(c0), tok(c0), tok(c0), gcs(c0), grs(c0), tok(c1), tok(c1), tok(c1), gcs(c1), grs(c1)],
        out_specs=[tok(c0), tok(c1)],
        out_shape=[o, o],
        scratch_shapes=[pltpu.VMEM((2, B_HEADS, B_HEAD_DIM, B_HEAD_DIM), F32)],
        compiler_params=_cparams(("parallel", "arbitrary")),
        name="gdn_scan",
    )(q, k, v, gcol, grow_c, q, k, v, gcol, grow_c)


def _conf_kernel(val_ref, gate_ref, dw_ref, o_ref, ctx_scr, w_scr, h_scr, *, ctx_len, n_rows):
    cblk = pl.program_id(1)
    K = DW_CONV
    half = K // 2
    pad = 2 * SUBLANES
    stride = GRID_W + 2 * pad
    u = val_ref[0].astype(F32) * _sigmoid(gate_ref[0].astype(F32))
    dw = dw_ref[...]

    ctx_scr[...] = jnp.zeros_like(ctx_scr)
    ctx_scr[pad:pad + ctx_len, :] = u[:ctx_len]
    acc = jnp.zeros((ctx_len, LANES), F32)
    for j in range(K):
        acc = acc + dw[j:j + 1] * ctx_scr[pad - half + j:pad - half + j + ctx_len, :]
    o_ref[0, :ctx_len, :] = acc.astype(o_ref.dtype)

    lat = u[ctx_len:]

    @pl.when(cblk < 2)
    def _():
        w_scr[...] = jnp.zeros_like(w_scr)
        for r in range(n_rows):
            w_scr[r * stride + pad:r * stride + pad + GRID_W, :] = lat[r * GRID_W:(r + 1) * GRID_W]
        for r in range(n_rows):
            a = jnp.zeros((GRID_W, LANES), F32)
            for j in range(K):
                o = r * stride + pad - half + j
                a = a + dw[j:j + 1] * w_scr[o:o + GRID_W, :]
            o_ref[0, ctx_len + r * GRID_W:ctx_len + (r + 1) * GRID_W, :] = a.astype(o_ref.dtype)

    @pl.when(cblk >= 2)
    def _():
        hp = half * GRID_W
        h_scr[...] = jnp.zeros_like(h_scr)
        h_scr[hp:hp + n_rows * GRID_W, :] = lat
        for r in range(n_rows):
            a = jnp.zeros((GRID_W, LANES), F32)
            for j in range(K):
                rr = r + j - half
                if 0 <= rr < n_rows:
                    a = a + dw[j:j + 1] * h_scr[hp + rr * GRID_W:hp + (rr + 1) * GRID_W, :]
            o_ref[0, ctx_len + r * GRID_W:ctx_len + (r + 1) * GRID_W, :] = a.astype(o_ref.dtype)


def _conformer_conv(P, dw, ctx_len):
    B, S, _ = P.shape
    W = BRANCH_WIDTH
    n_rows = (S - ctx_len) // GRID_W
    pad = 2 * SUBLANES
    nblk = W // LANES
    return pl.pallas_call(
        functools.partial(_conf_kernel, ctx_len=ctx_len, n_rows=n_rows),
        grid=(B, nblk),
        in_specs=[pl.BlockSpec((1, S, LANES), lambda b, c: (b, 0, COL_C // LANES + c)),
                  pl.BlockSpec((1, S, LANES), lambda b, c: (b, 0, (COL_C + W) // LANES + c)),
                  pl.BlockSpec((DW_CONV, LANES), lambda b, c: (0, c))],
        out_specs=pl.BlockSpec((1, S, LANES), lambda b, c: (b, 0, c)),
        out_shape=jax.ShapeDtypeStruct((B, S, W), BF16),
        scratch_shapes=[pltpu.VMEM((ctx_len + 2 * pad, LANES), F32),
                        pltpu.VMEM((n_rows * (GRID_W + 2 * pad), LANES), F32),
                        pltpu.VMEM(((n_rows + 2 * (DW_CONV // 2)) * GRID_W, LANES), F32)],
        compiler_params=_cparams(("parallel", "parallel")),
        name="conformer_conv",
    )(P, P, dw)


def _merge_kernel(x_ref, m2b_ref, m2c_ref, y0_ref, y1_ref, bonus_ref, g_ref, o0_ref, o1_ref, z_ref,
                  yc_ref, pg0_ref, pg1_ref, pg2_ref, lng_ref, lnb_ref, gng_ref, dwb_ref, cg_ref, cb_ref,
                  wb_ref, wo_ref, out_ref, *, ctx_len, skip):
    i = pl.program_id(1) + skip
    W = BRANCH_WIDTH
    D = x_ref.shape[-1]
    n = x_ref.shape[1]
    y = y0_ref[0].astype(F32) + y1_ref[0].astype(F32)
    gsum = _group_ones(W, A_HEAD_DIM)
    mu = _dot_sel_right(y, gsum) * (1.0 / A_HEAD_DIM)
    yc = y - mu
    var = _dot_sel_right(yc * yc, gsum) * (1.0 / A_HEAD_DIM)
    ya = (yc * lax.rsqrt(var + LN_X_EPS) * lng_ref[...] + lnb_ref[...] + bonus_ref[0].astype(F32)) * g_ref[0].astype(F32)
    o = o0_ref[0].astype(F32) + o1_ref[0].astype(F32)
    z = z_ref[0].astype(F32)
    parts = []
    for hd in range(B_HEADS):
        oh = o[:, hd * B_HEAD_DIM:(hd + 1) * B_HEAD_DIM]
        parts.append(oh * lax.rsqrt(jnp.mean(oh * oh, axis=-1, keepdims=True) + EPS) * gng_ref[...])
    yb = jnp.concatenate(parts, axis=1) * _silu(z)
    c = yc_ref[0].astype(F32) + dwb_ref[...]
    cm = jnp.mean(c, axis=-1, keepdims=True)
    cc = c - cm
    cv = jnp.mean(cc * cc, axis=-1, keepdims=True)
    ycf = _silu(cc * lax.rsqrt(cv + EPS) * cg_ref[...] + cb_ref[...])

    acc = jnp.zeros((n, D), F32)
    for nb, (br, pg_ref) in enumerate(((ya, pg0_ref), (yb, pg1_ref), (ycf, pg2_ref))):
        up = _dot(br.astype(BF16), wb_ref[nb])
        acc = acc + _sigmoid(pg_ref[0].astype(F32)) * up
    mix = _dot(acc.astype(BF16), wo_ref[...])
    row = i * n + lax.broadcasted_iota(I32, (n, 1), 0)
    m2 = jnp.where(row < ctx_len, m2c_ref[0], m2b_ref[0])
    out_ref[0] = x_ref[0] + m2 * mix


def _merge(x, mods, ctx_row, y0, y1, bonus, g, o0, o1, P, ycv, ln_g, ln_b, gdn_g, dw_b, c_g, c_b, wb, wo,
           ctx_len, latent_only):
    B, S, D = x.shape
    W = BRANCH_WIDTH
    nt = S // ROW_TILE
    skip = ctx_len // ROW_TILE if latent_only else 0
    assert COL_Z % W == 0 and COL_G % D == 0
    tokw = pl.BlockSpec((1, ROW_TILE, W), lambda b, i: (b, i + skip, 0))
    tokd = pl.BlockSpec((1, ROW_TILE, D), lambda b, i: (b, i + skip, 0))
    gate = lambda nb: pl.BlockSpec((1, ROW_TILE, D), lambda b, i: (b, i + skip, COL_G // D + nb))
    full = lambda shape: pl.BlockSpec(shape, lambda b, i: tuple(0 for _ in shape))
    return pl.pallas_call(
        functools.partial(_merge_kernel, ctx_len=ctx_len, skip=skip),
        grid=(B, nt - skip),
        in_specs=[tokd] + _mod_specs(2, ctx_row, (2,), D)
                 + [tokw, tokw, tokw, tokw, tokw, tokw,
                    pl.BlockSpec((1, ROW_TILE, W), lambda b, i: (b, i + skip, COL_Z // W)),
                    tokw, gate(0), gate(1), gate(2),
                    full((1, W)), full((1, W)), full((1, B_HEAD_DIM)), full((1, W)), full((1, W)),
                    full((1, W)), full((3, W, D)), full((D, D))],
        out_specs=pl.BlockSpec((1, ROW_TILE, D), lambda b, i: (b, i, 0)) if latent_only else tokd,
        out_shape=jax.ShapeDtypeStruct((B, S - skip * ROW_TILE, D), F32),
        input_output_aliases={} if latent_only else {0: 0},
        compiler_params=_cparams(("parallel", "parallel")),
        name="merge",
    )(x, mods, mods, y0, y1, bonus, g, o0, o1, P, ycv, P, P, P,
      ln_g.reshape(1, W), ln_b.reshape(1, W), gdn_g.reshape(1, B_HEAD_DIM), dw_b.reshape(1, W),
      c_g.reshape(1, W), c_b.reshape(1, W), wb, wo)


def _ffn_pre_kernel(x_ref, g_ref, shb_ref, scb_ref, shc_ref, scc_ref, wr_ref, rb_ref, h_ref, e_ref, w_ref,
                    *, ctx_len):
    i = pl.program_id(1)
    n = x_ref.shape[1]
    mods = (shb_ref[0], scb_ref[0], shc_ref[0], scc_ref[0])
    h = _modulated(x_ref[0], g_ref[...], mods, i * n, ctx_len)
    _store_token_tiles(h_ref, h)
    logits = lax.dot_general(wr_ref[...], h, (((1,), (1,)), ((), ())),
                             preferred_element_type=F32, precision=HIGHEST)
    scores = 1.0 / (1.0 + jnp.exp(-logits))
    sel = scores + rb_ref[...]
    rows = [sel[e:e + 1, :] for e in range(N_EXPERTS)]
    srow = [scores[e:e + 1, :] for e in range(N_EXPERTS)]
    gscore = []
    for gi in range(N_GROUPS):
        m = rows[gi * EXP_PER_GROUP:(gi + 1) * EXP_PER_GROUP]
        best = None
        for a in range(EXP_PER_GROUP):
            for b in range(a + 1, EXP_PER_GROUP):
                t = m[a] + m[b]
                best = t if best is None else jnp.maximum(best, t)
        gscore.append(best)
    grp = jnp.zeros_like(gscore[0], dtype=I32)
    gbest = gscore[0]
    for gi in range(1, N_GROUPS):
        better = gscore[gi] > gbest
        grp = jnp.where(better, gi, grp)
        gbest = jnp.where(better, gscore[gi], gbest)
    mem = []
    mem_s = []
    for a in range(EXP_PER_GROUP):
        va = rows[a]
        sa = srow[a]
        for gi in range(1, N_GROUPS):
            va = jnp.where(grp == gi, rows[gi * EXP_PER_GROUP + a], va)
            sa = jnp.where(grp == gi, srow[gi * EXP_PER_GROUP + a], sa)
        mem.append(va)
        mem_s.append(sa)
    neg = jnp.full_like(mem[0], -jnp.inf)
    i1 = jnp.zeros_like(grp)
    b1 = mem[0]
    for a in range(1, EXP_PER_GROUP):
        better = mem[a] > b1
        i1 = jnp.where(better, a, i1)
        b1 = jnp.where(better, mem[a], b1)
    i2 = jnp.zeros_like(grp)
    b2 = neg
    first = True
    for a in range(EXP_PER_GROUP):
        cand = jnp.where(i1 == a, neg, mem[a])
        if first:
            b2 = cand
            first = False
        else:
            better = cand > b2
            i2 = jnp.where(better, a, i2)
            b2 = jnp.where(better, cand, b2)
    s1 = mem_s[0]
    s2 = mem_s[0]
    for a in range(1, EXP_PER_GROUP):
        s1 = jnp.where(i1 == a, mem_s[a], s1)
        s2 = jnp.where(i2 == a, mem_s[a], s2)
    tot = s1 + s2
    e_ref[0, 0:1, :] = grp * EXP_PER_GROUP + i1
    e_ref[0, 1:2, :] = grp * EXP_PER_GROUP + i2
    w_ref[0, 0:1, :] = s1 / tot
    w_ref[0, 1:2, :] = s2 / tot


def _ffn_pre(x, gain, mods, ctx_row, w_router, router_bias, ctx_len):
    B, S, D = x.shape
    nt = S // ROW_TILE
    tokd = pl.BlockSpec((1, ROW_TILE, D), lambda b, i: (b, i, 0))
    full = lambda shape: pl.BlockSpec(shape, lambda b, i: tuple(0 for _ in shape))
    sel = pl.BlockSpec((1, 2, ROW_TILE), lambda b, i: (b * nt + i, 0, 0))
    return pl.pallas_call(
        functools.partial(_ffn_pre_kernel, ctx_len=ctx_len),
        grid=(B, nt),
        in_specs=[tokd, full((1, D))] + _mod_specs(2, ctx_row, (3, 4), D)
                 + [full((N_EXPERTS, D)), full((N_EXPERTS, 1))],
        out_specs=[pl.BlockSpec((ROW_TILE, SUBLANES, D // SUBLANES), lambda b, i: (b * nt + i, 0, 0)), sel, sel],
        out_shape=[jax.ShapeDtypeStruct((B * S, SUBLANES, D // SUBLANES), F32),
                   jax.ShapeDtypeStruct((B * nt, 2, ROW_TILE), I32),
                   jax.ShapeDtypeStruct((B * nt, 2, ROW_TILE), F32)],
        compiler_params=_cparams(("parallel", "parallel")),
        name="ffn_pre",
    )(x, gain.reshape(1, D), mods, mods, mods, mods, w_router.T, router_bias.reshape(N_EXPERTS, 1))


def _moe_dispatch(e, w, n_tok):
    R = MOE_ROWS
    A = 2 * n_tok
    ef = e.reshape(A)
    wf = w.reshape(A)
    onehot = (ef[:, None] == jnp.arange(N_EXPERTS, dtype=I32)[None, :]).astype(I32)
    csum = jnp.cumsum(onehot, axis=0)
    rank = jnp.take_along_axis(csum, ef[:, None], axis=1)[:, 0] - 1
    sizes = csum[-1]
    padded = (sizes + R - 1) // R * R
    pad_ends = jnp.cumsum(padded)
    pad_starts = pad_ends - padded
    dest = pad_starts[ef] + rank
    n_blocks = -(-A // R) + N_EXPERTS
    rows = n_blocks * R
    pair = jnp.arange(A, dtype=I32)
    ridx = jnp.arange(rows, dtype=I32)
    table = jnp.stack([A + ridx % R, jnp.zeros((rows,), I32)], axis=1)
    table = table.at[dest].set(jnp.stack([pair, lax.bitcast_convert_type(wf, I32)], axis=1))
    row_dst = table[:, 0]
    row_w = lax.bitcast_convert_type(table[:, 1], F32)
    row_src = jnp.where(row_dst >= A, 0, jnp.where(row_dst >= n_tok, row_dst - n_tok, row_dst))
    block_e = jnp.minimum(jnp.searchsorted(pad_ends, jnp.arange(n_blocks, dtype=I32) * R, side='right'),
                          N_EXPERTS - 1).astype(I32)
    return row_src.reshape(n_blocks, 1, R), row_dst.reshape(n_blocks, 1, R), row_w.reshape(rows, 1), block_e


def _moe_kernel(be_ref, src_ref, nsrc_ref, dst_ref, rw_ref, h_hbm, wg_ref, wu_ref, wd_ref, y_hbm,
                xbuf, ybuf, sem_in, sem_out):
    del be_ref
    R = MOE_ROWS
    b = pl.program_id(0)
    nb = pl.num_programs(0)
    slot = lax.rem(b, 2)
    other = 1 - slot

    def row_in(idx_ref, r, sl):
        return pltpu.make_async_copy(h_hbm.at[idx_ref[0, 0, r]],
                                     xbuf.at[sl, r // SUBLANES, :, r % SUBLANES, :], sem_in.at[sl])

    def row_out(r, sl):
        return pltpu.make_async_copy(ybuf.at[sl, r // SUBLANES, :, r % SUBLANES, :],
                                     y_hbm.at[dst_ref[0, 0, r]], sem_out.at[sl])

    @pl.when(b == 0)
    def _():
        for r in range(R):
            row_in(src_ref, r, slot).start()

    for r in range(R):
        row_in(src_ref, r, slot).wait()
    for r in range(R):
        row_in(nsrc_ref, r, other).start(priority=r % 2)
    nchunk = xbuf.shape[2]
    lanes = xbuf.shape[4]
    xb = jnp.concatenate([xbuf[slot, :, c, :, :].reshape(R, lanes) for c in range(nchunk)],
                         axis=1).astype(BF16)
    gate = _dot(xb, wg_ref[0])
    up = _dot(xb, wu_ref[0])
    act = (_silu(gate) * up).astype(BF16)
    y = _dot(act, wd_ref[0]) * rw_ref[...]
    for c in range(nchunk):
        ybuf[slot, :, c, :, :] = y[:, c * lanes:(c + 1) * lanes].reshape(R // SUBLANES, SUBLANES, lanes)

    @pl.when(b >= 1)
    def _():
        for r in range(R):
            row_out(r, other).wait()

    for r in range(R):
        row_out(r, slot).start(priority=r % 2)

    @pl.when(b == nb - 1)
    def _():
        for r in range(R):
            row_out(r, slot).wait()
        for r in range(R):
            row_in(nsrc_ref, r, other).wait()


def _moe(h2, row_src, row_dst, row_w, block_e, wg, wu, wd):
    n_tok, ts, tw = h2.shape
    D = ts * tw
    n_blocks = block_e.shape[0]
    R = MOE_ROWS
    F = wg.shape[-1]
    grid_spec = pltpu.PrefetchScalarGridSpec(
        num_scalar_prefetch=1,
        grid=(n_blocks,),
        in_specs=[pl.BlockSpec((1, 1, R), lambda b, be: (b, 0, 0), memory_space=pltpu.SMEM),
                  pl.BlockSpec((1, 1, R), lambda b, be: (jnp.minimum(b + 1, n_blocks - 1), 0, 0),
                               memory_space=pltpu.SMEM),
                  pl.BlockSpec((1, 1, R), lambda b, be: (b, 0, 0), memory_space=pltpu.SMEM),
                  pl.BlockSpec((R, 1), lambda b, be: (b, 0)),
                  pl.BlockSpec(memory_space=pl.ANY),
                  pl.BlockSpec((1, D, F), lambda b, be: (be[b], 0, 0)),
                  pl.BlockSpec((1, D, F), lambda b, be: (be[b], 0, 0)),
                  pl.BlockSpec((1, F, D), lambda b, be: (be[b], 0, 0))],
        out_specs=pl.BlockSpec(memory_space=pl.ANY),
        scratch_shapes=[pltpu.VMEM((2, R // SUBLANES, ts, SUBLANES, tw), F32),
                        pltpu.VMEM((2, R // SUBLANES, ts, SUBLANES, tw), F32),
                        pltpu.SemaphoreType.DMA((2,)), pltpu.SemaphoreType.DMA((2,))],
    )
    return pl.pallas_call(
        _moe_kernel,
        grid_spec=grid_spec,
        out_shape=jax.ShapeDtypeStruct((2 * n_tok + R, ts, tw), F32),
        compiler_params=_cparams(("arbitrary",)),
        name="moe",
    )(block_e, row_src, row_src, row_dst, row_w, h2, wg, wu, wd)


def _ffn_add_kernel(x_ref, m5b_ref, m5c_ref, y0_ref, y1_ref, o_ref, *, ctx_len):
    i = pl.program_id(1)
    n = x_ref.shape[1]
    row = i * n + lax.broadcasted_iota(I32, (n, 1), 0)
    m5 = jnp.where(row < ctx_len, m5c_ref[0], m5b_ref[0])
    o_ref[0] = x_ref[0] + m5 * (_load_token_tiles(y0_ref) + _load_token_tiles(y1_ref))


def _ffn_add(x, mods, ctx_row, y, ctx_len):
    B, S, D = x.shape
    nt = S // ROW_TILE
    tokd = pl.BlockSpec((1, ROW_TILE, D), lambda b, i: (b, i, 0))
    yspec = lambda slot: pl.BlockSpec((ROW_TILE, SUBLANES, D // SUBLANES),
                                      lambda b, i: (slot * B * nt + b * nt + i, 0, 0))
    return pl.pallas_call(
        functools.partial(_ffn_add_kernel, ctx_len=ctx_len),
        grid=(B, nt),
        in_specs=[tokd] + _mod_specs(2, ctx_row, (5,), D) + [yspec(0), yspec(1)],
        out_specs=tokd,
        out_shape=jax.ShapeDtypeStruct((B, S, D), F32),
        input_output_aliases={0: 0},
        compiler_params=_cparams(("parallel", "parallel")),
        name="ffn_add",
    )(x, mods, mods, y, y)


def _final_kernel(x_ref, m5_ref, y0_ref, y1_ref, g_ref, o_ref):
    x = x_ref[0] + m5_ref[0] * (_load_token_tiles(y0_ref) + _load_token_tiles(y1_ref))
    o_ref[0] = x * lax.rsqrt(jnp.mean(x * x, axis=-1, keepdims=True) + EPS) * g_ref[...]


def _final(x, mods, y, gain, ctx_len):
    B, S, D = x.shape
    nt = S // ROW_TILE
    skip = ctx_len // ROW_TILE
    tok_in = pl.BlockSpec((1, ROW_TILE, D), lambda b, i: (b, i + skip, 0))
    yspec = lambda slot: pl.BlockSpec((ROW_TILE, SUBLANES, D // SUBLANES),
                                      lambda b, i: (slot * B * nt + b * nt + i + skip, 0, 0))
    return pl.pallas_call(
        _final_kernel,
        grid=(B, nt - skip),
        in_specs=[tok_in, pl.BlockSpec((1, 1, D), lambda b, i: (b, 0, 5)), yspec(0), yspec(1),
                  pl.BlockSpec((1, D), lambda b, i: (0, 0))],
        out_specs=pl.BlockSpec((1, ROW_TILE, D), lambda b, i: (b, i, 0)),
        out_shape=jax.ShapeDtypeStruct((B, S - ctx_len, D), F32),
        compiler_params=_cparams(("parallel", "parallel")),
        name="final_norm",
    )(x, mods, y, y, gain.reshape(1, D))


def _pad_w_in(w):
    W = BRANCH_WIDTH
    a = w[:, :A_COLS]
    qkvz = w[:, A_COLS:A_COLS + 4 * W]
    ab = w[:, A_COLS + 4 * W:A_COLS + 4 * W + 4 * B_HEADS]
    rest = w[:, A_COLS + 4 * W + 4 * B_HEADS:]
    ab = jnp.pad(ab, ((0, 0), (0, LANES - 4 * B_HEADS)))
    out = jnp.concatenate([a, ab, qkvz, rest], axis=1).astype(BF16)
    assert out.shape[1] == N_IN_PAD
    return out


def kernel(x, c, ctx, c_ctx, w_ada, b_ada, norm_mix, norm_ffn, norm_final, w_in, rwkv_mu, rwkv_w0, rwkv_w2, rwkv_a0, rwkv_a2, rwkv_g2, rwkv_kk, rwkv_ka, rwkv_rk, rwkv_ln_g, rwkv_ln_b, gdn_conv, gdn_A_log, gdn_dt_bias, gdn_norm, conf_dw, conf_dw_b, conf_ln_g, conf_ln_b, w_branch, w_out, w_router, router_bias, w_e_gate, w_e_up, w_e_down):
    B, T, D = x.shape
    ctx_len = ctx.shape[1]
    depth = w_in.shape[0]
    assert ctx_len == ROW_TILE and T % ROW_TILE == 0 and T % GRID_W == 0
    S = ctx_len + T
    n_tok = B * S

    xs = jnp.concatenate([ctx, x], axis=1)
    rows = -(-(B + 1) // SUBLANES) * SUBLANES
    svec = jnp.concatenate([c, c_ctx[None, :], jnp.zeros((rows - B - 1, D), F32)], axis=0)
    mods_all = _ada(svec, w_ada, b_ada)
    ctx_row = B

    out = None
    for l in range(depth):
        mods = mods_all[l].reshape(rows, 1, 6 * D)
        P = _inproj(xs, norm_mix[l], mods, ctx_row, _pad_w_in(w_in[l]), ctx_len)
        prep = _rwkv_prep(P, rwkv_mu[l], rwkv_w0[l], rwkv_w2[l], rwkv_a0[l], rwkv_a2[l], rwkv_g2[l],
                          rwkv_kk[l], rwkv_ka[l], rwkv_rk[l])
        bonus, gate = prep[11], prep[12]
        y0, y1 = _rwkv_scan(prep[:11], ctx_len)
        q, k, v, gcol, grow = _gdn_prep(P, gdn_conv[l], gdn_A_log[l], gdn_dt_bias[l])
        o0, o1 = _gdn_scan(q, k, v, gcol, grow, ctx_len)
        ycv = _conformer_conv(P, conf_dw[l], ctx_len)
        last = l == depth - 1
        xs = _merge(xs, mods, ctx_row, y0, y1, bonus, gate, o0, o1, P, ycv,
                    rwkv_ln_g[l], rwkv_ln_b[l], gdn_norm[l], conf_dw_b[l], conf_ln_g[l], conf_ln_b[l],
                    w_branch[l].astype(BF16), w_out[l].astype(BF16), ctx_len, latent_only=last)
        ctx_rows = 0 if last else ctx_len
        n_rows = xs.shape[0] * xs.shape[1]
        h2, e, w = _ffn_pre(xs, norm_ffn[l], mods, ctx_row, w_router, router_bias, ctx_rows)
        e = e.transpose(1, 0, 2).reshape(2, n_rows)
        w = w.transpose(1, 0, 2).reshape(2, n_rows)
        row_src, row_dst, row_w, block_e = _moe_dispatch(e, w, n_rows)
        y = _moe(h2, row_src, row_dst, row_w, block_e,
                 w_e_gate[l].astype(BF16), w_e_up[l].astype(BF16), w_e_down[l].astype(BF16))
        if last:
            out = _final(xs, mods, y, norm_final, ctx_rows)
        else:
            xs = _ffn_add(xs, mods, ctx_row, y, ctx_len)
    return out
```

```python
import functools
import math

import jax
import jax.numpy as jnp
from jax import lax
from jax.experimental import pallas as pl
from jax.experimental.pallas import tpu as pltpu

F32 = jnp.float32
BF16 = jnp.bfloat16
I32 = jnp.int32
HIGHEST = lax.Precision.HIGHEST

EPS = 1e-6
LN_X_EPS = 64e-5
GRID_W = 64
BRANCH_WIDTH = 512
A_HEADS = 8
A_HEAD_DIM = 64
B_HEADS = 4
B_HEAD_DIM = 128
DECAY_LORA = 64
SHORT_CONV = 5
DW_CONV = 31
N_EXPERTS = 16
N_GROUPS = 4
EXP_PER_GROUP = 4
D_EXPERT = 512

ROW_TILE = 256
CHUNK = 64
RWKV_SCAN_CHUNKS = 4
MOE_ROWS = 256
LANES = 128
SUBLANES = 8
HALO_ROWS = 16
VMEM_LIMIT = 56 * 1024 * 1024

A_COLS = 1920
COL_A = 0
COL_AB = 1920
COL_Q = 2048
COL_Z = 3584
COL_C = 4096
COL_G = 5120
N_IN_PAD = 8192
LORA_W = 3 * LANES


def _cparams(sem):
    return pltpu.CompilerParams(dimension_semantics=sem, vmem_limit_bytes=VMEM_LIMIT)


def _sigmoid(x):
    return 0.5 * jnp.tanh(0.5 * x) + 0.5


def _silu(x):
    return x * _sigmoid(x)


def _softplus(x):
    return jnp.maximum(x, 0.0) + jnp.log(1.0 + jnp.exp(-jnp.abs(x)))


def _dot(a, b):
    return jnp.dot(a, b, preferred_element_type=F32)


def _dot_nt(a, b):
    return lax.dot_general(a, b, (((1,), (1,)), ((), ())), preferred_element_type=F32)


def _bf16_parts(x, n):
    parts = []
    r = x
    for _ in range(n):
        p = r.astype(BF16)
        parts.append(p)
        r = r - p.astype(F32)
    return parts


def _dot_sel_left(m, x, n=3):
    acc = None
    for p in _bf16_parts(x, n):
        t = _dot(m, p)
        acc = t if acc is None else acc + t
    return acc


def _dot_sel_right(x, m, n=2):
    acc = None
    for p in _bf16_parts(x, n):
        t = _dot(p, m)
        acc = t if acc is None else acc + t
    return acc


def _group_ones(n, group):
    r = lax.broadcasted_iota(I32, (n, n), 0) // group
    c = lax.broadcasted_iota(I32, (n, n), 1) // group
    return jnp.where(r == c, 1.0, 0.0).astype(BF16)


def _chunk_tri(n, lower):
    r = lax.broadcasted_iota(I32, (n, n), 0)
    c = lax.broadcasted_iota(I32, (n, n), 1)
    same = (r // CHUNK) == (c // CHUNK)
    keep = (c <= r) if lower else (c >= r)
    return jnp.where(same & keep, 1.0, 0.0).astype(BF16)


def _shift_down(x, prev_rows, s):
    n = x.shape[0]
    hr = prev_rows.shape[0]
    out = pltpu.roll(x, s, axis=0)
    row = lax.broadcasted_iota(I32, (n, 1), 0)
    for q in range(s):
        out = jnp.where(row == q, prev_rows[hr - s + q:hr - s + q + 1, :], out)
    return out


def _shift_up(x, next_rows, s):
    n = x.shape[0]
    out = pltpu.roll(x, n - s, axis=0)
    row = lax.broadcasted_iota(I32, (n, 1), 0)
    for q in range(s):
        out = jnp.where(row == n - s + q, next_rows[q:q + 1, :], out)
    return out


def _row_shifts(x_bf16, prev_rows, next_rows, offsets):
    n = x_bf16.shape[0]
    hr = prev_rows.shape[0]
    r = lax.broadcasted_iota(I32, (n, n), 0)
    c = lax.broadcasted_iota(I32, (n, n), 1)
    sel = jnp.concatenate([jnp.where(c == r + off, 1.0, 0.0).astype(BF16) for off in offsets], axis=0)
    sh = _dot(sel, x_bf16)
    row8 = lax.broadcasted_iota(I32, (SUBLANES, 1), 0)
    outs = []
    for q, off in enumerate(offsets):
        o = sh[q * n:(q + 1) * n]
        s = abs(off)
        assert 0 < s <= SUBLANES
        if off < 0:
            edge = o[:SUBLANES]
            for t in range(s):
                edge = jnp.where(row8 == t, prev_rows[hr - s + t:hr - s + t + 1, :], edge)
            o = jnp.concatenate([edge, o[SUBLANES:]], axis=0)
        else:
            edge = o[n - SUBLANES:]
            for t in range(s):
                edge = jnp.where(row8 == SUBLANES - s + t, next_rows[t:t + 1, :], edge)
            o = jnp.concatenate([o[:n - SUBLANES], edge], axis=0)
        outs.append(o)
    return outs


def _halo_rows(p_prev_ref, p_next_ref, i, n_tiles):
    prev_rows = p_prev_ref[0].astype(F32)
    next_rows = p_next_ref[0].astype(F32)
    prev_rows = jnp.where(i <= 1, 0.0, prev_rows)
    next_rows = jnp.where((i == 0) | (i == n_tiles - 1), 0.0, next_rows)
    return prev_rows, next_rows


def _halo_specs(width, col, tiles_per_batch):
    assert col % width == 0
    col_block = col // width
    sub_per_tile = ROW_TILE // HALO_ROWS
    last = tiles_per_batch * sub_per_tile - 1
    main = pl.BlockSpec((1, ROW_TILE, width), lambda b, i: (b, i, col_block))
    prev = pl.BlockSpec((1, HALO_ROWS, width),
                        lambda b, i: (b, jnp.maximum(i * sub_per_tile - 1, 0), col_block))
    nxt = pl.BlockSpec((1, HALO_ROWS, width),
                       lambda b, i: (b, jnp.minimum((i + 1) * sub_per_tile, last), col_block))
    return [main, prev, nxt]


def _ada_kernel(s_ref, w_ref, b_ref, o_ref):
    s = s_ref[...]
    o_ref[0] = jnp.dot(_silu(s), w_ref[0], preferred_element_type=F32, precision=HIGHEST) + b_ref[0]


def _ada(svec, w_ada, b_ada):
    L, D, N = w_ada.shape
    rows = svec.shape[0]
    tn = 1536
    return pl.pallas_call(
        _ada_kernel,
        grid=(L, N // tn),
        in_specs=[pl.BlockSpec((rows, D), lambda l, j: (0, 0)),
                  pl.BlockSpec((1, D, tn), lambda l, j: (l, 0, j)),
                  pl.BlockSpec((1, 1, tn), lambda l, j: (l, 0, j))],
        out_specs=pl.BlockSpec((1, rows, tn), lambda l, j: (l, 0, j)),
        out_shape=jax.ShapeDtypeStruct((L, rows, N), F32),
        compiler_params=_cparams(("parallel", "parallel")),
        name="ada",
    )(svec, w_ada, b_ada.reshape(L, 1, N))


def _store_token_tiles(ref, x):
    w = x.shape[1] // SUBLANES
    for s in range(SUBLANES):
        ref[:, s, :] = x[:, s * w:(s + 1) * w]


def _load_token_tiles(ref):
    return jnp.concatenate([ref[:, s, :] for s in range(SUBLANES)], axis=1)


def _modulated(x, gain, mods, row0, ctx_len):
    sh_b, sc_b, sh_c, sc_c = mods
    y = x * lax.rsqrt(jnp.mean(x * x, axis=-1, keepdims=True) + EPS) * gain
    row = row0 + lax.broadcasted_iota(I32, (x.shape[0], 1), 0)
    isc = row < ctx_len
    shift = jnp.where(isc, sh_c, sh_b)
    scale = jnp.where(isc, sc_c, sc_b)
    return y * (1.0 + scale) + shift


def _mod_specs(n_grid, ctx_row, chunks, d):
    def spec(row_of, c):
        if n_grid == 2:
            return pl.BlockSpec((1, 1, d), lambda b, i: (row_of(b), 0, c))
        return pl.BlockSpec((1, 1, d), lambda b, i, j: (row_of(b), 0, c))
    return ([spec(lambda b: b, c) for c in chunks] + [spec(lambda b: ctx_row, c) for c in chunks])


def _inproj_kernel(x_ref, g_ref, shb_ref, scb_ref, shc_ref, scc_ref, w_ref, o_ref, h_scr, *, tm, ctx_len):
    i = pl.program_id(1)
    j = pl.program_id(2)

    @pl.when(j == 0)
    def _():
        mods = (shb_ref[0], scb_ref[0], shc_ref[0], scc_ref[0])
        h_scr[...] = _modulated(x_ref[0], g_ref[...], mods, i * tm, ctx_len).astype(BF16)

    o_ref[0] = _dot(h_scr[...], w_ref[...]).astype(o_ref.dtype)


def _inproj(x, gain, mods, ctx_row, wp, ctx_len):
    B, S, D = x.shape
    N = wp.shape[1]
    tm = S // 2
    tn = 1024
    return pl.pallas_call(
        functools.partial(_inproj_kernel, tm=tm, ctx_len=ctx_len),
        grid=(B, S // tm, N // tn),
        in_specs=[pl.BlockSpec((1, tm, D), lambda b, i, j: (b, i, 0)),
                  pl.BlockSpec((1, D), lambda b, i, j: (0, 0))]
                 + _mod_specs(3, ctx_row, (0, 1), D)
                 + [pl.BlockSpec((D, tn), lambda b, i, j: (0, j))],
        out_specs=pl.BlockSpec((1, tm, tn), lambda b, i, j: (b, i, j)),
        out_shape=jax.ShapeDtypeStruct((B, S, N), BF16),
        scratch_shapes=[pltpu.VMEM((tm, D), BF16)],
        compiler_params=_cparams(("parallel", "parallel", "arbitrary")),
        name="inproj",
    )(x, gain.reshape(1, D), mods, mods, mods, mods, wp)


def _rwkv_prep_kernel(r_ref, rp_ref, rn_ref, k_ref, kp_ref_, kn_ref, vv_ref, vp_ref, vn_ref,
                      x_ref, xp_ref, xn_ref,
                      mu_ref, w0_ref, w2_ref, a0_ref, a2_ref, g2_ref, kk_ref, ka_ref, rk_ref,
                      rt0_ref, kp0_ref, bt0_ref, kt0_ref, pc0_ref,
                      rt1_ref, kp1_ref, bt1_ref, kt1_ref, pc1_ref,
                      v_ref, bonus_ref, g_ref, *, n_tiles):
    i = pl.program_id(1)
    W = BRANCH_WIDTH

    def shifted(main, prv, nxt, col, width):
        p = main[0].astype(F32)
        prev_rows, next_rows = _halo_rows(prv, nxt, i, n_tiles)
        mu = mu_ref[:, col:col + width]
        return (p + mu[0:1] * (_shift_down(p, prev_rows, 1) - p)
                + mu[1:2] * (_shift_up(p, next_rows, 1) - p))

    r = shifted(r_ref, rp_ref, rn_ref, 0, W)
    k = shifted(k_ref, kp_ref_, kn_ref, W, W)
    v = shifted(vv_ref, vp_ref, vn_ref, 2 * W, W)
    lora = shifted(x_ref, xp_ref, xn_ref, 3 * W, LORA_W)
    xw = jnp.tanh(lora[:, 0:LANES])
    xa = lora[:, LANES:2 * LANES]
    xg = _sigmoid(lora[:, 2 * LANES:3 * LANES])
    lane_hi = lax.broadcasted_iota(I32, xw.shape, 1) >= DECAY_LORA

    g_ref[0] = _dot(xg.astype(BF16), g2_ref[...].astype(BF16)).astype(g_ref.dtype)
    v_ref[0] = v.astype(BF16)

    gsum = _group_ones(W, A_HEAD_DIM)
    kx = k * kk_ref[...]
    kk = kx * lax.rsqrt(_dot_sel_right(kx * kx, gsum) + EPS)
    bonus_ref[0] = (_dot_sel_right(r * k * rk_ref[...], gsum) * v).astype(bonus_ref.dtype)

    w2 = w2_ref[...].astype(BF16)
    a2 = a2_ref[...].astype(BF16)
    n = r.shape[0]
    outs = ((rt0_ref, kp0_ref, bt0_ref, kt0_ref, pc0_ref), (rt1_ref, kp1_ref, bt1_ref, kt1_ref, pc1_ref))
    for d in range(2):
        keep = lane_hi if d == 1 else jnp.logical_not(lane_hi)
        lw = w0_ref[d:d + 1, :] + _dot(jnp.where(keep, xw, 0.0).astype(BF16), w2)
        ew = _sigmoid(lw) * math.exp(-0.5)
        a = _sigmoid(a0_ref[d:d + 1, :] + _dot(jnp.where(keep, xa, 0.0).astype(BF16), a2))
        keff = k * (1.0 + (a - 1.0) * ka_ref[...])
        cs = _dot_sel_left(_chunk_tri(n, lower=(d == 0)), ew, n=2)
        p_incl = jnp.exp(-cs)
        p_excl = jnp.exp(ew - cs)
        p_inv = jnp.exp(cs)
        rt_ref, kp_ref, bt_ref, kt_ref, pc_ref = outs[d]
        rt_ref[0] = (r * p_incl).astype(BF16)
        kp_ref[0] = (kk * p_excl).astype(BF16)
        bt_ref[0] = (kk * a * p_inv).astype(BF16)
        kt_ref[0] = (keff * p_inv).astype(BF16)
        for c in range(n // CHUNK):
            row = c * CHUNK + (CHUNK - 1 if d == 0 else 0)
            pc_ref[0, c] = p_incl[row:row + 1, :]


def _rwkv_prep(P, mu, w0, w2, a0, a2, g2, k_k, k_a, r_k):
    B, S, _ = P.shape
    W = BRANCH_WIDTH
    nt = S // ROW_TILE
    cpt = ROW_TILE // CHUNK
    halo = (_halo_specs(W, COL_A, nt) + _halo_specs(W, COL_A + W, nt) + _halo_specs(W, COL_A + 2 * W, nt)
            + _halo_specs(LORA_W, COL_A + 3 * W, nt))
    full = lambda shape: pl.BlockSpec(shape, lambda b, i: tuple(0 for _ in shape))
    tok = pl.BlockSpec((1, ROW_TILE, W), lambda b, i: (b, i, 0))
    pcs = pl.BlockSpec((1, cpt, 1, W), lambda b, i: (b, i, 0, 0))
    bf = jax.ShapeDtypeStruct((B, S, W), BF16)
    f32 = jax.ShapeDtypeStruct((B, S, W), F32)
    pcshape = jax.ShapeDtypeStruct((B, S // CHUNK, 1, W), F32)
    return pl.pallas_call(
        functools.partial(_rwkv_prep_kernel, n_tiles=nt),
        grid=(B, nt),
        in_specs=halo + [full((2, A_COLS)), full((2, W)), full((2 * DECAY_LORA, W)), full((2, W)),
                         full((2 * DECAY_LORA, W)), full((LANES, W)), full((1, W)), full((1, W)),
                         full((1, W))],
        out_specs=[tok, tok, tok, tok, pcs, tok, tok, tok, tok, pcs, tok, tok, tok],
        out_shape=[bf, bf, bf, bf, pcshape, bf, bf, bf, bf, pcshape, bf, bf, bf],
        compiler_params=_cparams(("parallel", "parallel")),
        name="rwkv_prep",
    )(*([P] * 12), mu, w0, w2.reshape(2 * DECAY_LORA, W), a0, a2.reshape(2 * DECAY_LORA, W), g2,
      k_k.reshape(1, W), k_a.reshape(1, W), r_k.reshape(1, W))


def _unit_lower_inverse(a_neg, steps):
    n = a_neg.shape[0]
    eye = jnp.where(lax.broadcasted_iota(I32, (n, n), 0) == lax.broadcasted_iota(I32, (n, n), 1), 1.0, 0.0)
    t = eye + a_neg
    xp = a_neg.astype(BF16)
    for _ in range(steps - 1):
        x2 = _dot(xp, xp)
        xp = x2.astype(BF16)
        t = t + _dot(t.astype(BF16), xp)
    return t


def _run_chains(chains):
    while chains:
        alive = []
        for g in chains:
            try:
                next(g)
                alive.append(g)
            except StopIteration:
                pass
        chains = alive


def _stack_heads(x, lane_hi):
    z = jnp.zeros_like(x)
    return jnp.concatenate([jnp.where(lane_hi, z, x), jnp.where(lane_hi, x, z)], axis=0)


def _rwkv_scan_kernel(rt0, kp0, bt0, kt0, v0, pc0, rt1, kp1, bt1, kt1, v1, pc1, y0_ref, y1_ref, st_ref):
    s = pl.program_id(1)

    @pl.when(s == 0)
    def _():
        st_ref[...] = jnp.zeros_like(st_ref)

    C = CHUNK
    n2 = 2 * C
    ri = lax.broadcasted_iota(I32, (n2, n2), 0)
    ci = lax.broadcasted_iota(I32, (n2, n2), 1)
    same = (ri // C) == (ci // C)
    tr = ri % C
    ts = ci % C
    lane_hi = lax.broadcasted_iota(I32, (C, n2), 1) >= C
    eye = jnp.where(ri == ci, 1.0, 0.0)
    dirs = ((rt0, kp0, bt0, kt0, v0, pc0, y0_ref), (rt1, kp1, bt1, kt1, v1, pc1, y1_ref))
    nch = rt0.shape[1] // C
    ready = {}

    def intra(d, j, cc):
        rt, kp, bt, kt, vv, pc, _ = dirs[d]
        strict = same & ((ts < tr) if d == 0 else (ts > tr))
        incl = same & ((ts <= tr) if d == 0 else (ts >= tr))
        sl = slice(n2 * j, n2 * (j + 1))
        rows = slice(cc * C, (cc + 1) * C)
        kap = kp[0, rows, sl]
        r_ = rt[0, rows, sl]
        b_ = bt[0, rows, sl]
        k_ = kt[0, rows, sl]
        v_st = _stack_heads(vv[0, rows, sl], lane_hi)
        aa = _dot_nt(jnp.concatenate([_stack_heads(kap, lane_hi), _stack_heads(r_, lane_hi)], axis=0),
                     jnp.concatenate([b_, b_, k_, k_], axis=0))
        yield
        x = jnp.where(strict, -aa[:n2, :n2], 0.0)
        a_ak = jnp.where(strict, aa[:n2, n2:], 0.0).astype(BF16)
        a_rb = jnp.where(incl, aa[n2:, :n2], 0.0).astype(BF16)
        a_rk = jnp.where(incl, aa[n2:, n2:], 0.0).astype(BF16)
        t = eye + x
        xp = x.astype(BF16)
        b_t = _stack_heads(b_, lane_hi).astype(F32).T.astype(BF16)
        k_t = _stack_heads(k_, lane_hi).astype(F32).T.astype(BF16)
        sv = _dot(jnp.concatenate([a_ak, a_rk, k_t], axis=0), v_st)
        akv = sv[:n2]
        arkv = sv[n2:2 * n2]
        kv0 = sv[2 * n2:]
        x2 = _dot(xp, xp)
        yield
        for _ in range(4):
            xp = x2.astype(BF16)
            tx = _dot(jnp.concatenate([t.astype(BF16), xp], axis=0), xp)
            yield
            t = t + tx[:n2]
            x2 = tx[n2:]
        t = t + _dot(t.astype(BF16), x2.astype(BF16))
        yield
        pcm = jnp.broadcast_to(pc[0, cc][:, sl], (n2, n2)).T
        ready[(d, j, cc)] = (jnp.concatenate([kap, r_], axis=0), t.astype(BF16), akv, arkv, kv0,
                             jnp.concatenate([a_rb, b_t], axis=0), pcm)

    def state(d, j):
        y_ref = dirs[d][6]
        sl = slice(n2 * j, n2 * (j + 1))
        h = st_ref[d, j]
        for cc in (range(nch) if d == 0 else range(nch - 1, -1, -1)):
            while (d, j, cc) not in ready:
                yield
            kap_r, t_b, akv, arkv, kv0, arb_bt, pcm = ready.pop((d, j, cc))
            krh = _dot(kap_r, h.astype(BF16))
            yield
            rhs = akv + _stack_heads(krh[:C], lane_hi)
            u_st = -_dot(t_b, rhs.astype(BF16))
            yield
            yd = _dot(arb_bt, u_st.astype(BF16))
            yield
            y_st = yd[:n2] + arkv
            y_ref[0, cc * C:(cc + 1) * C, sl] = (y_st[:C] + y_st[C:] + krh[C:]).astype(y_ref.dtype)
            h = pcm * (h + kv0 + yd[n2:])
        st_ref[d, j] = h

    units = [(d, j) for d in range(2) for j in range(BRANCH_WIDTH // n2)]
    chains = []
    for pos in range(nch):
        for d, j in units:
            chains.append(intra(d, j, pos if d == 0 else nch - 1 - pos))
    _run_chains(chains + [state(d, j) for d, j in units])


def _scan_tile_maps(nt_ctx, nt):
    def t0(s):
        return s

    def t1(s):
        return jnp.where(s < nt_ctx, nt_ctx - 1 - s, nt + nt_ctx - 1 - s)

    return t0, t1


def _rwkv_scan(prep, ctx_len):
    rt0, kp0, bt0, kt0, pc0, rt1, kp1, bt1, kt1, pc1, v = prep
    B, S, W = v.shape
    rows = RWKV_SCAN_CHUNKS * CHUNK
    c0, c1 = _scan_tile_maps(ctx_len // rows, S // rows)
    tok = lambda cm: pl.BlockSpec((1, rows, W), lambda b, s: (b, cm(s), 0))
    pcs = lambda cm: pl.BlockSpec((1, RWKV_SCAN_CHUNKS, 1, W), lambda b, s: (b, cm(s), 0, 0))
    y = jax.ShapeDtypeStruct((B, S, W), BF16)
    return pl.pallas_call(
        _rwkv_scan_kernel,
        grid=(B, S // rows),
        in_specs=[tok(c0), tok(c0), tok(c0), tok(c0), tok(c0), pcs(c0),
                  tok(c1), tok(c1), tok(c1), tok(c1), tok(c1), pcs(c1)],
        out_specs=[tok(c0), tok(c1)],
        out_shape=[y, y],
        scratch_shapes=[pltpu.VMEM((2, W // (2 * CHUNK), 2 * CHUNK, 2 * CHUNK), F32)],
        compiler_params=_cparams(("parallel", "arbitrary")),
        name="rwkv_scan",
    )(rt0, kp0, bt0, kt0, v, pc0, rt1, kp1, bt1, kt1, v, pc1)


def _gdn_prep_kernel(qi_ref, qp_ref, qn_ref, ki_ref, kp_ref, kn_ref, vi_ref, vp_ref, vn_ref,
                     ab_ref, cw_ref, al_ref, dt_ref,
                     q_ref, k_ref, v_ref, gcol_ref, grow_ref, *, n_tiles):
    i = pl.program_id(1)
    W = BRANCH_WIDTH
    half = SHORT_CONV // 2

    def conv_silu(main, prv, nxt, col):
        x = main[0].astype(F32)
        prev_rows, next_rows = _halo_rows(prv, nxt, i, n_tiles)
        cw = cw_ref[:, col:col + W]
        offsets = tuple(o for o in range(-half, half + 1) if o != 0)
        acc = cw[half:half + 1] * x
        for off, xs in zip(offsets, _row_shifts(main[0], prev_rows, next_rows, offsets)):
            acc = acc + cw[half + off:half + off + 1] * xs
        return _silu(acc)

    yq = conv_silu(qi_ref, qp_ref, qn_ref, 0)
    yk = conv_silu(ki_ref, kp_ref, kn_ref, W)
    v_ref[0] = conv_silu(vi_ref, vp_ref, vn_ref, 2 * W).astype(v_ref.dtype)
    for hd in range(B_HEADS):
        sl = slice(hd * B_HEAD_DIM, (hd + 1) * B_HEAD_DIM)
        q = yq[:, sl]
        q_ref[0, :, sl] = (q * lax.rsqrt(jnp.sum(q * q, axis=-1, keepdims=True) + EPS)
                           * (B_HEAD_DIM ** -0.5)).astype(q_ref.dtype)
        k = yk[:, sl]
        k_ref[0, :, sl] = (k * lax.rsqrt(jnp.sum(k * k, axis=-1, keepdims=True) + EPS)).astype(k_ref.dtype)

    ab = ab_ref[0].astype(F32)
    n = ab.shape[0]
    lane = lax.broadcasted_iota(I32, ab.shape, 1)
    g = -jnp.exp(al_ref[...]) * _softplus(ab + dt_ref[...])
    g = jnp.where(lane < 2 * B_HEADS, g, 0.0)
    gc_f = _dot_sel_left(_chunk_tri(n, lower=True), g)
    gc_b = _dot_sel_left(_chunk_tri(n, lower=False), g)
    gc = jnp.where(lane < B_HEADS, gc_f, gc_b)
    beta = _sigmoid(ab)
    out = jnp.where(lane < 2 * B_HEADS, gc, jnp.where(lane < 4 * B_HEADS, beta, 0.0))
    gcol_ref[0] = out
    grow_ref[0] = out.T[:4 * B_HEADS, :]


def _gdn_prep(P, conv_w, A_log, dt_bias):
    B, S, _ = P.shape
    W = BRANCH_WIDTH
    nt = S // ROW_TILE
    halo = _halo_specs(W, COL_Q, nt) + _halo_specs(W, COL_Q + W, nt) + _halo_specs(W, COL_Q + 2 * W, nt)
    pad = jnp.zeros((1, LANES - 2 * B_HEADS), F32)
    al = jnp.concatenate([A_log.reshape(1, 2 * B_HEADS), pad], axis=1)
    dt = jnp.concatenate([dt_bias.reshape(1, 2 * B_HEADS), pad], axis=1)
    full = lambda shape: pl.BlockSpec(shape, lambda b, i: tuple(0 for _ in shape))
    tok = pl.BlockSpec((1, ROW_TILE, W), lambda b, i: (b, i, 0))
    bf16 = jax.ShapeDtypeStruct((B, S, W), BF16)
    return pl.pallas_call(
        functools.partial(_gdn_prep_kernel, n_tiles=nt),
        grid=(B, nt),
        in_specs=halo + [pl.BlockSpec((1, ROW_TILE, LANES), lambda b, i: (b, i, COL_AB // LANES)),
                         full((SHORT_CONV, 3 * W)), full((1, LANES)), full((1, LANES))],
        out_specs=[tok, tok, tok,
                   pl.BlockSpec((1, ROW_TILE, LANES), lambda b, i: (b, i, 0)),
                   pl.BlockSpec((1, 4 * B_HEADS, ROW_TILE), lambda b, i: (b, 0, i))],
        out_shape=[bf16, bf16, bf16, jax.ShapeDtypeStruct((B, S, LANES), F32),
                   jax.ShapeDtypeStruct((B, 4 * B_HEADS, S), F32)],
        compiler_params=_cparams(("parallel", "parallel")),
        name="gdn_prep",
    )(*([P] * 10), conv_w, al, dt)


def _gdn_scan_kernel(q0, k0, v0, gc0, gr0, q1, k1, v1, gc1, gr1, o0_ref, o1_ref, st_ref):
    s = pl.program_id(1)

    @pl.when(s == 0)
    def _():
        st_ref[...] = jnp.zeros_like(st_ref)

    C = CHUNK
    nch = q0.shape[1] // C
    ri = lax.broadcasted_iota(I32, (C, C), 0)
    ci = lax.broadcasted_iota(I32, (C, C), 1)
    dirs = ((q0, k0, v0, gc0, gr0, o0_ref), (q1, k1, v1, gc1, gr1, o1_ref))
    eye = jnp.where(ri == ci, 1.0, 0.0)
    ready = {}

    def intra(d, hd, cc):
        qr, kr, vr, gcr, grr, _ = dirs[d]
        incl = (ci <= ri) if d == 0 else (ci >= ri)
        strict = (ci < ri) if d == 0 else (ci > ri)
        last = C - 1 if d == 0 else 0
        rows = slice(cc * C, (cc + 1) * C)
        gcol = gcr[0, rows, :]
        grow = grr[0, cc]
        sl = slice(hd * B_HEAD_DIM, (hd + 1) * B_HEAD_DIM)
        idx = d * B_HEADS + hd
        q = qr[0, rows, sl].astype(F32)
        k = kr[0, rows, sl].astype(F32)
        v = vr[0, rows, sl].astype(F32)
        gc_c = gcol[:, idx:idx + 1]
        gc_r = grow[idx:idx + 1, :]
        beta = gcol[:, 2 * B_HEADS + idx:2 * B_HEADS + idx + 1]
        gc_last = gc_r[:, last:last + 1]
        diff = gc_c - gc_r
        decay_incl = jnp.where(incl, jnp.exp(jnp.where(incl, diff, 0.0)), 0.0)
        decay_strict = jnp.where(strict, decay_incl, 0.0)
        kb = k * beta
        e_gc = jnp.exp(gc_c)
        qk = _dot_nt(jnp.concatenate([kb, q], axis=0).astype(BF16), k.astype(BF16))
        yield
        x = -(qk[:C] * decay_strict)
        attn = (qk[C:] * decay_incl).astype(BF16)
        rhs = jnp.concatenate([v * beta, kb * e_gc], axis=1).astype(BF16)
        q_dec = (q * e_gc).astype(BF16)
        k_dec_t = (k * jnp.exp(gc_last - gc_c)).T.astype(BF16)
        g_last = jnp.exp(gc_last)
        t = eye + x
        xp = x.astype(BF16)
        x2 = _dot(xp, xp)
        yield
        for _ in range(4):
            xp = x2.astype(BF16)
            tx = _dot(jnp.concatenate([t.astype(BF16), xp], axis=0), xp)
            yield
            t = t + tx[:C]
            x2 = tx[C:]
        t = t + _dot(t.astype(BF16), x2.astype(BF16))
        yield
        sol = _dot(t.astype(BF16), rhs)
        yield
        ready[(d, hd, cc)] = (sol[:, :B_HEAD_DIM], sol[:, B_HEAD_DIM:].astype(BF16), q_dec, attn, k_dec_t,
                              g_last)

    def state(d, hd):
        o_ref = dirs[d][5]
        sl = slice(hd * B_HEAD_DIM, (hd + 1) * B_HEAD_DIM)
        st = st_ref[d, hd]
        for cc in (range(nch) if d == 0 else range(nch - 1, -1, -1)):
            while (d, hd, cc) not in ready:
                yield
            u, w, q_dec, attn, k_dec_t, g_last = ready.pop((d, hd, cc))
            ws_qs = _dot(jnp.concatenate([w, q_dec], axis=0), st.astype(BF16))
            yield
            vn_b = (u - ws_qs[:C]).astype(BF16)
            od = _dot(jnp.concatenate([attn, k_dec_t], axis=0), vn_b)
            yield
            o_ref[0, cc * C:(cc + 1) * C, sl] = (ws_qs[C:] + od[:C]).astype(o_ref.dtype)
            st = st * g_last + od[C:]
        st_ref[d, hd] = st

    units = [(d, hd) for d in range(2) for hd in range(B_HEADS)]
    chains = []
    for pos in range(nch):
        for d, hd in units:
            chains.append(intra(d, hd, pos if d == 0 else nch - 1 - pos))
    _run_chains(chains + [state(d, hd) for d, hd in units])


def _gdn_scan(q, k, v, gcol, grow, ctx_len):
    B, S, W = q.shape
    nc = S // CHUNK
    nt = S // ROW_TILE
    cpt = ROW_TILE // CHUNK
    c0, c1 = _scan_tile_maps(ctx_len // ROW_TILE, nt)
    tok = lambda cm: pl.BlockSpec((1, ROW_TILE, W), lambda b, s: (b, cm(s), 0))
    gcs = lambda cm: pl.BlockSpec((1, ROW_TILE, LANES), lambda b, s: (b, cm(s), 0))
    grs = lambda cm: pl.BlockSpec((1, cpt, 4 * B_HEADS, CHUNK), lambda b, s: (b, cm(s), 0, 0))
    grow_c = grow.reshape(B, 4 * B_HEADS, nc, CHUNK).transpose(0, 2, 1, 3)
    o = jax.ShapeDtypeStruct((B, S, W), BF16)
    return pl.pallas_call(
        _gdn_scan_kernel,
        grid=(B, nt),
        in_specs=[tok(c0), tok(c0), tok(c0), gcs(c0), grs(c0), tok(c1), tok(c1), tok(c1), gcs(c1), grs(c1)],
        out_specs=[tok(c0), tok(c1)],
        out_shape=[o, o],
        scratch_shapes=[pltpu.VMEM((2, B_HEADS, B_HEAD_DIM, B_HEAD_DIM), F32)],
        compiler_params=_cparams(("parallel", "arbitrary")),
        name="gdn_scan",
    )(q, k, v, gcol, grow_c, q, k, v, gcol, grow_c)


def _conf_kernel(val_ref, gate_ref, dw_ref, o_ref, ctx_scr, w_scr, h_scr, *, ctx_len, n_rows):
    cblk = pl.program_id(1)
    K = DW_CONV
    half = K // 2
    pad = 2 * SUBLANES
    stride = GRID_W + 2 * pad
    u = val_ref[0].astype(F32) * _sigmoid(gate_ref[0].astype(F32))
    dw = dw_ref[...]

    ctx_scr[...] = jnp.zeros_like(ctx_scr)
    ctx_scr[pad:pad + ctx_len, :] = u[:ctx_len]
    acc = jnp.zeros((ctx_len, LANES), F32)
    for j in range(K):
        acc = acc + dw[j:j + 1] * ctx_scr[pad - half + j:pad - half + j + ctx_len, :]
    o_ref[0, :ctx_len, :] = acc.astype(o_ref.dtype)

    lat = u[ctx_len:]

    @pl.when(cblk < 2)
    def _():
        w_scr[...] = jnp.zeros_like(w_scr)
        for r in range(n_rows):
            w_scr[r * stride + pad:r * stride + pad + GRID_W, :] = lat[r * GRID_W:(r + 1) * GRID_W]
        for r in range(n_rows):
            a = jnp.zeros((GRID_W, LANES), F32)
            for j in range(K):
                o = r * stride + pad - half + j
                a = a + dw[j:j + 1] * w_scr[o:o + GRID_W, :]
            o_ref[0, ctx_len + r * GRID_W:ctx_len + (r + 1) * GRID_W, :] = a.astype(o_ref.dtype)

    @pl.when(cblk >= 2)
    def _():
        hp = half * GRID_W
        h_scr[...] = jnp.zeros_like(h_scr)
        h_scr[hp:hp + n_rows * GRID_W, :] = lat
        for r in range(n_rows):
            a = jnp.zeros((GRID_W, LANES), F32)
            for j in range(K):
                rr = r + j - half
                if 0 <= rr < n_rows:
                    a = a + dw[j:j + 1] * h_scr[hp + rr * GRID_W:hp + (rr + 1) * GRID_W, :]
            o_ref[0, ctx_len + r * GRID_W:ctx_len + (r + 1) * GRID_W, :] = a.astype(o_ref.dtype)


def _conformer_conv(P, dw, ctx_len):
    B, S, _ = P.shape
    W = BRANCH_WIDTH
    n_rows = (S - ctx_len) // GRID_W
    pad = 2 * SUBLANES
    nblk = W // LANES
    return pl.pallas_call(
        functools.partial(_conf_kernel, ctx_len=ctx_len, n_rows=n_rows),
        grid=(B, nblk),
        in_specs=[pl.BlockSpec((1, S, LANES), lambda b, c: (b, 0, COL_C // LANES + c)),
                  pl.BlockSpec((1, S, LANES), lambda b, c: (b, 0, (COL_C + W) // LANES + c)),
                  pl.BlockSpec((DW_CONV, LANES), lambda b, c: (0, c))],
        out_specs=pl.BlockSpec((1, S, LANES), lambda b, c: (b, 0, c)),
        out_shape=jax.ShapeDtypeStruct((B, S, W), BF16),
        scratch_shapes=[pltpu.VMEM((ctx_len + 2 * pad, LANES), F32),
                        pltpu.VMEM((n_rows * (GRID_W + 2 * pad), LANES), F32),
                        pltpu.VMEM(((n_rows + 2 * (DW_CONV // 2)) * GRID_W, LANES), F32)],
        compiler_params=_cparams(("parallel", "parallel")),
        name="conformer_conv",
    )(P, P, dw)


def _merge_kernel(x_ref, m2b_ref, m2c_ref, y0_ref, y1_ref, bonus_ref, g_ref, o0_ref, o1_ref, z_ref,
                  yc_ref, pg0_ref, pg1_ref, pg2_ref, lng_ref, lnb_ref, gng_ref, dwb_ref, cg_ref, cb_ref,
                  wb_ref, wo_ref, out_ref, *, ctx_len, skip):
    i = pl.program_id(1) + skip
    W = BRANCH_WIDTH
    D = x_ref.shape[-1]
    n = x_ref.shape[1]
    y = y0_ref[0].astype(F32) + y1_ref[0].astype(F32)
    gsum = _group_ones(W, A_HEAD_DIM)
    mu = _dot_sel_right(y, gsum) * (1.0 / A_HEAD_DIM)
    yc = y - mu
    var = _dot_sel_right(yc * yc, gsum) * (1.0 / A_HEAD_DIM)
    ya = (yc * lax.rsqrt(var + LN_X_EPS) * lng_ref[...] + lnb_ref[...] + bonus_ref[0].astype(F32)) * g_ref[0].astype(F32)
    o = o0_ref[0].astype(F32) + o1_ref[0].astype(F32)
    z = z_ref[0].astype(F32)
    parts = []
    for hd in range(B_HEADS):
        oh = o[:, hd * B_HEAD_DIM:(hd + 1) * B_HEAD_DIM]
        parts.append(oh * lax.rsqrt(jnp.mean(oh * oh, axis=-1, keepdims=True) + EPS) * gng_ref[...])
    yb = jnp.concatenate(parts, axis=1) * _silu(z)
    c = yc_ref[0].astype(F32) + dwb_ref[...]
    cm = jnp.mean(c, axis=-1, keepdims=True)
    cc = c - cm
    cv = jnp.mean(cc * cc, axis=-1, keepdims=True)
    ycf = _silu(cc * lax.rsqrt(cv + EPS) * cg_ref[...] + cb_ref[...])

    acc = jnp.zeros((n, D), F32)
    for nb, (br, pg_ref) in enumerate(((ya, pg0_ref), (yb, pg1_ref), (ycf, pg2_ref))):
        up = _dot(br.astype(BF16), wb_ref[nb])
        acc = acc + _sigmoid(pg_ref[0].astype(F32)) * up
    mix = _dot(acc.astype(BF16), wo_ref[...])
    row = i * n + lax.broadcasted_iota(I32, (n, 1), 0)
    m2 = jnp.where(row < ctx_len, m2c_ref[0], m2b_ref[0])
    out_ref[0] = x_ref[0] + m2 * mix


def _merge(x, mods, ctx_row, y0, y1, bonus, g, o0, o1, P, ycv, ln_g, ln_b, gdn_g, dw_b, c_g, c_b, wb, wo,
           ctx_len, latent_only):
    B, S, D = x.shape
    W = BRANCH_WIDTH
    nt = S // ROW_TILE
    skip = ctx_len // ROW_TILE if latent_only else 0
    assert COL_Z % W == 0 and COL_G % D == 0
    tokw = pl.BlockSpec((1, ROW_TILE, W), lambda b, i: (b, i + skip, 0))
    tokd = pl.BlockSpec((1, ROW_TILE, D), lambda b, i: (b, i + skip, 0))
    gate = lambda nb: pl.BlockSpec((1, ROW_TILE, D), lambda b, i: (b, i + skip, COL_G // D + nb))
    full = lambda shape: pl.BlockSpec(shape, lambda b, i: tuple(0 for _ in shape))
    return pl.pallas_call(
        functools.partial(_merge_kernel, ctx_len=ctx_len, skip=skip),
        grid=(B, nt - skip),
        in_specs=[tokd] + _mod_specs(2, ctx_row, (2,), D)
                 + [tokw, tokw, tokw, tokw, tokw, tokw,
                    pl.BlockSpec((1, ROW_TILE, W), lambda b, i: (b, i + skip, COL_Z // W)),
                    tokw, gate(0), gate(1), gate(2),
                    full((1, W)), full((1, W)), full((1, B_HEAD_DIM)), full((1, W)), full((1, W)),
                    full((1, W)), full((3, W, D)), full((D, D))],
        out_specs=pl.BlockSpec((1, ROW_TILE, D), lambda b, i: (b, i, 0)) if latent_only else tokd,
        out_shape=jax.ShapeDtypeStruct((B, S - skip * ROW_TILE, D), F32),
        input_output_aliases={} if latent_only else {0: 0},
        compiler_params=_cparams(("parallel", "parallel")),
        name="merge",
    )(x, mods, mods, y0, y1, bonus, g, o0, o1, P, ycv, P, P, P,
      ln_g.reshape(1, W), ln_b.reshape(1, W), gdn_g.reshape(1, B_HEAD_DIM), dw_b.reshape(1, W),
      c_g.reshape(1, W), c_b.reshape(1, W), wb, wo)


def _ffn_pre_kernel(x_ref, g_ref, shb_ref, scb_ref, shc_ref, scc_ref, wr_ref, rb_ref, h_ref, e_ref, w_ref,
                    *, ctx_len):
    i = pl.program_id(1)
    n = x_ref.shape[1]
    mods = (shb_ref[0], scb_ref[0], shc_ref[0], scc_ref[0])
    h = _modulated(x_ref[0], g_ref[...], mods, i * n, ctx_len)
    _store_token_tiles(h_ref, h)
    logits = lax.dot_general(wr_ref[...], h, (((1,), (1,)), ((), ())),
                             preferred_element_type=F32, precision=HIGHEST)
    scores = 1.0 / (1.0 + jnp.exp(-logits))
    sel = scores + rb_ref[...]
    rows = [sel[e:e + 1, :] for e in range(N_EXPERTS)]
    srow = [scores[e:e + 1, :] for e in range(N_EXPERTS)]
    gscore = []
    for gi in range(N_GROUPS):
        m = rows[gi * EXP_PER_GROUP:(gi + 1) * EXP_PER_GROUP]
        best = None
        for a in range(EXP_PER_GROUP):
            for b in range(a + 1, EXP_PER_GROUP):
                t = m[a] + m[b]
                best = t if best is None else jnp.maximum(best, t)
        gscore.append(best)
    grp = jnp.zeros_like(gscore[0], dtype=I32)
    gbest = gscore[0]
    for gi in range(1, N_GROUPS):
        better = gscore[gi] > gbest
        grp = jnp.where(better, gi, grp)
        gbest = jnp.where(better, gscore[gi], gbest)
    mem = []
    mem_s = []
    for a in range(EXP_PER_GROUP):
        va = rows[a]
        sa = srow[a]
        for gi in range(1, N_GROUPS):
            va = jnp.where(grp == gi, rows[gi * EXP_PER_GROUP + a], va)
            sa = jnp.where(grp == gi, srow[gi * EXP_PER_GROUP + a], sa)
        mem.append(va)
        mem_s.append(sa)
    neg = jnp.full_like(mem[0], -jnp.inf)
    i1 = jnp.zeros_like(grp)
    b1 = mem[0]
    for a in range(1, EXP_PER_GROUP):
        better = mem[a] > b1
        i1 = jnp.where(better, a, i1)
        b1 = jnp.where(better, mem[a], b1)
    i2 = jnp.zeros_like(grp)
    b2 = neg
    first = True
    for a in range(EXP_PER_GROUP):
        cand = jnp.where(i1 == a, neg, mem[a])
        if first:
            b2 = cand
            first = False
        else:
            better = cand > b2
            i2 = jnp.where(better, a, i2)
            b2 = jnp.where(better, cand, b2)
    s1 = mem_s[0]
    s2 = mem_s[0]
    for a in range(1, EXP_PER_GROUP):
        s1 = jnp.where(i1 == a, mem_s[a], s1)
        s2 = jnp.where(i2 == a, mem_s[a], s2)
    tot = s1 + s2
    e_ref[0, 0:1, :] = grp * EXP_PER_GROUP + i1
    e_ref[0, 1:2, :] = grp * EXP_PER_GROUP + i2
    w_ref[0, 0:1, :] = s1 / tot
    w_ref[0, 1:2, :] = s2 / tot


def _ffn_pre(x, gain, mods, ctx_row, w_router, router_bias, ctx_len):
    B, S, D = x.shape
    nt = S // ROW_TILE
    tokd = pl.BlockSpec((1, ROW_TILE, D), lambda b, i: (b, i, 0))
    full = lambda shape: pl.BlockSpec(shape, lambda b, i: tuple(0 for _ in shape))
    sel = pl.BlockSpec((1, 2, ROW_TILE), lambda b, i: (b * nt + i, 0, 0))
    return pl.pallas_call(
        functools.partial(_ffn_pre_kernel, ctx_len=ctx_len),
        grid=(B, nt),
        in_specs=[tokd, full((1, D))] + _mod_specs(2, ctx_row, (3, 4), D)
                 + [full((N_EXPERTS, D)), full((N_EXPERTS, 1))],
        out_specs=[pl.BlockSpec((ROW_TILE, SUBLANES, D // SUBLANES), lambda b, i: (b * nt + i, 0, 0)), sel, sel],
        out_shape=[jax.ShapeDtypeStruct((B * S, SUBLANES, D // SUBLANES), F32),
                   jax.ShapeDtypeStruct((B * nt, 2, ROW_TILE), I32),
                   jax.ShapeDtypeStruct((B * nt, 2, ROW_TILE), F32)],
        compiler_params=_cparams(("parallel", "parallel")),
        name="ffn_pre",
    )(x, gain.reshape(1, D), mods, mods, mods, mods, w_router.T, router_bias.reshape(N_EXPERTS, 1))


def _moe_dispatch(e, w, n_tok):
    R = MOE_ROWS
    A = 2 * n_tok
    ef = e.reshape(A)
    wf = w.reshape(A)
    onehot = (ef[:, None] == jnp.arange(N_EXPERTS, dtype=I32)[None, :]).astype(I32)
    csum = jnp.cumsum(onehot, axis=0)
    rank = jnp.take_along_axis(csum, ef[:, None], axis=1)[:, 0] - 1
    sizes = csum[-1]
    padded = (sizes + R - 1) // R * R
    pad_ends = jnp.cumsum(padded)
    pad_starts = pad_ends - padded
    dest = pad_starts[ef] + rank
    n_blocks = -(-A // R) + N_EXPERTS
    rows = n_blocks * R
    pair = jnp.arange(A, dtype=I32)
    ridx = jnp.arange(rows, dtype=I32)
    table = jnp.stack([A + ridx % R, jnp.zeros((rows,), I32)], axis=1)
    table = table.at[dest].set(jnp.stack([pair, lax.bitcast_convert_type(wf, I32)], axis=1))
    row_dst = table[:, 0]
    row_w = lax.bitcast_convert_type(table[:, 1], F32)
    row_src = jnp.where(row_dst >= A, 0, jnp.where(row_dst >= n_tok, row_dst - n_tok, row_dst))
    block_e = jnp.minimum(jnp.searchsorted(pad_ends, jnp.arange(n_blocks, dtype=I32) * R, side='right'),
                          N_EXPERTS - 1).astype(I32)
    return row_src.reshape(n_blocks, 1, R), row_dst.reshape(n_blocks, 1, R), row_w.reshape(rows, 1), block_e


def _moe_kernel(be_ref, src_ref, nsrc_ref, dst_ref, rw_ref, h_hbm, wg_ref, wu_ref, wd_ref, y_hbm,
                xbuf, ybuf, sem_in, sem_out):
    del be_ref
    R = MOE_ROWS
    b = pl.program_id(0)
    nb = pl.num_programs(0)
    slot = lax.rem(b, 2)
    other = 1 - slot

    def row_in(idx_ref, r, sl):
        return pltpu.make_async_copy(h_hbm.at[idx_ref[0, 0, r]],
                                     xbuf.at[sl, r // SUBLANES, :, r % SUBLANES, :], sem_in.at[sl])

    def row_out(r, sl):
        return pltpu.make_async_copy(ybuf.at[sl, r // SUBLANES, :, r % SUBLANES, :],
                                     y_hbm.at[dst_ref[0, 0, r]], sem_out.at[sl])

    @pl.when(b == 0)
    def _():
        for r in range(R):
            row_in(src_ref, r, slot).start()

    for r in range(R):
        row_in(src_ref, r, slot).wait()
    for r in range(R):
        row_in(nsrc_ref, r, other).start(priority=r % 2)
    nchunk = xbuf.shape[2]
    lanes = xbuf.shape[4]
    xb = jnp.concatenate([xbuf[slot, :, c, :, :].reshape(R, lanes) for c in range(nchunk)],
                         axis=1).astype(BF16)
    gate = _dot(xb, wg_ref[0])
    up = _dot(xb, wu_ref[0])
    act = (_silu(gate) * up).astype(BF16)
    y = _dot(act, wd_ref[0]) * rw_ref[...]
    for c in range(nchunk):
        ybuf[slot, :, c, :, :] = y[:, c * lanes:(c + 1) * lanes].reshape(R // SUBLANES, SUBLANES, lanes)

    @pl.when(b >= 1)
    def _():
        for r in range(R):
            row_out(r, other).wait()

    for r in range(R):
        row_out(r, slot).start(priority=r % 2)

    @pl.when(b == nb - 1)
    def _():
        for r in range(R):
            row_out(r, slot).wait()
        for r in range(R):
            row_in(nsrc_ref, r, other).wait()


def _moe(h2, row_src, row_dst, row_w, block_e, wg, wu, wd):
    n_tok, ts, tw = h2.shape
    D = ts * tw
    n_blocks = block_e.shape[0]
    R = MOE_ROWS
    F = wg.shape[-1]
    grid_spec = pltpu.PrefetchScalarGridSpec(
        num_scalar_prefetch=1,
        grid=(n_blocks,),
        in_specs=[pl.BlockSpec((1, 1, R), lambda b, be: (b, 0, 0), memory_space=pltpu.SMEM),
                  pl.BlockSpec((1, 1, R), lambda b, be: (jnp.minimum(b + 1, n_blocks - 1), 0, 0),
                               memory_space=pltpu.SMEM),
                  pl.BlockSpec((1, 1, R), lambda b, be: (b, 0, 0), memory_space=pltpu.SMEM),
                  pl.BlockSpec((R, 1), lambda b, be: (b, 0)),
                  pl.BlockSpec(memory_space=pl.ANY),
                  pl.BlockSpec((1, D, F), lambda b, be: (be[b], 0, 0)),
                  pl.BlockSpec((1, D, F), lambda b, be: (be[b], 0, 0)),
                  pl.BlockSpec((1, F, D), lambda b, be: (be[b], 0, 0))],
        out_specs=pl.BlockSpec(memory_space=pl.ANY),
        scratch_shapes=[pltpu.VMEM((2, R // SUBLANES, ts, SUBLANES, tw), F32),
                        pltpu.VMEM((2, R // SUBLANES, ts, SUBLANES, tw), F32),
                        pltpu.SemaphoreType.DMA((2,)), pltpu.SemaphoreType.DMA((2,))],
    )
    return pl.pallas_call(
        _moe_kernel,
        grid_spec=grid_spec,
        out_shape=jax.ShapeDtypeStruct((2 * n_tok + R, ts, tw), F32),
        compiler_params=_cparams(("arbitrary",)),
        name="moe",
    )(block_e, row_src, row_src, row_dst, row_w, h2, wg, wu, wd)


def _ffn_add_kernel(x_ref, m5b_ref, m5c_ref, y0_ref, y1_ref, o_ref, *, ctx_len):
    i = pl.program_id(1)
    n = x_ref.shape[1]
    row = i * n + lax.broadcasted_iota(I32, (n, 1), 0)
    m5 = jnp.where(row < ctx_len, m5c_ref[0], m5b_ref[0])
    o_ref[0] = x_ref[0] + m5 * (_load_token_tiles(y0_ref) + _load_token_tiles(y1_ref))


def _ffn_add(x, mods, ctx_row, y, ctx_len):
    B, S, D = x.shape
    nt = S // ROW_TILE
    tokd = pl.BlockSpec((1, ROW_TILE, D), lambda b, i: (b, i, 0))
    yspec = lambda slot: pl.BlockSpec((ROW_TILE, SUBLANES, D // SUBLANES),
                                      lambda b, i: (slot * B * nt + b * nt + i, 0, 0))
    return pl.pallas_call(
        functools.partial(_ffn_add_kernel, ctx_len=ctx_len),
        grid=(B, nt),
        in_specs=[tokd] + _mod_specs(2, ctx_row, (5,), D) + [yspec(0), yspec(1)],
        out_specs=tokd,
        out_shape=jax.ShapeDtypeStruct((B, S, D), F32),
        input_output_aliases={0: 0},
        compiler_params=_cparams(("parallel", "parallel")),
        name="ffn_add",
    )(x, mods, mods, y, y)


def _final_kernel(x_ref, m5_ref, y0_ref, y1_ref, g_ref, o_ref):
    x = x_ref[0] + m5_ref[0] * (_load_token_tiles(y0_ref) + _load_token_tiles(y1_ref))
    o_ref[0] = x * lax.rsqrt(jnp.mean(x * x, axis=-1, keepdims=True) + EPS) * g_ref[...]


def _final(x, mods, y, gain, ctx_len):
    B, S, D = x.shape
    nt = S // ROW_TILE
    skip = ctx_len // ROW_TILE
    tok_in = pl.BlockSpec((1, ROW_TILE, D), lambda b, i: (b, i + skip, 0))
    yspec = lambda slot: pl.BlockSpec((ROW_TILE, SUBLANES, D // SUBLANES),
                                      lambda b, i: (slot * B * nt + b * nt + i + skip, 0, 0))
    return pl.pallas_call(
        _final_kernel,
        grid=(B, nt - skip),
        in_specs=[tok_in, pl.BlockSpec((1, 1, D), lambda b, i: (b, 0, 5)), yspec(0), yspec(1),
                  pl.BlockSpec((1, D), lambda b, i: (0, 0))],
        out_specs=pl.BlockSpec((1, ROW_TILE, D), lambda b, i: (b, i, 0)),
        out_shape=jax.ShapeDtypeStruct((B, S - ctx_len, D), F32),
        compiler_params=_cparams(("parallel", "parallel")),
        name="final_norm",
    )(x, mods, y, y, gain.reshape(1, D))


def _pad_w_in(w):
    W = BRANCH_WIDTH
    a = w[:, :A_COLS]
    qkvz = w[:, A_COLS:A_COLS + 4 * W]
    ab = w[:, A_COLS + 4 * W:A_COLS + 4 * W + 4 * B_HEADS]
    rest = w[:, A_COLS + 4 * W + 4 * B_HEADS:]
    ab = jnp.pad(ab, ((0, 0), (0, LANES - 4 * B_HEADS)))
    out = jnp.concatenate([a, ab, qkvz, rest], axis=1).astype(BF16)
    assert out.shape[1] == N_IN_PAD
    return out


def kernel(x, c, ctx, c_ctx, w_ada, b_ada, norm_mix, norm_ffn, norm_final, w_in, rwkv_mu, rwkv_w0, rwkv_w2, rwkv_a0, rwkv_a2, rwkv_g2, rwkv_kk, rwkv_ka, rwkv_rk, rwkv_ln_g, rwkv_ln_b, gdn_conv, gdn_A_log, gdn_dt_bias, gdn_norm, conf_dw, conf_dw_b, conf_ln_g, conf_ln_b, w_branch, w_out, w_router, router_bias, w_e_gate, w_e_up, w_e_down):
    B, T, D = x.shape
    ctx_len = ctx.shape[1]
    depth = w_in.shape[0]
    assert ctx_len == ROW_TILE and T % ROW_TILE == 0 and T % GRID_W == 0
    S = ctx_len + T
    n_tok = B * S

    xs = jnp.concatenate([ctx, x], axis=1)
    rows = -(-(B + 1) // SUBLANES) * SUBLANES
    svec = jnp.concatenate([c, c_ctx[None, :], jnp.zeros((rows - B - 1, D), F32)], axis=0)
    mods_all = _ada(svec, w_ada, b_ada)
    ctx_row = B

    out = None
    for l in range(depth):
        mods = mods_all[l].reshape(rows, 1, 6 * D)
        P = _inproj(xs, norm_mix[l], mods, ctx_row, _pad_w_in(w_in[l]), ctx_len)
        prep = _rwkv_prep(P, rwkv_mu[l], rwkv_w0[l], rwkv_w2[l], rwkv_a0[l], rwkv_a2[l], rwkv_g2[l],
                          rwkv_kk[l], rwkv_ka[l], rwkv_rk[l])
        bonus, gate = prep[11], prep[12]
        y0, y1 = _rwkv_scan(prep[:11], ctx_len)
        q, k, v, gcol, grow = _gdn_prep(P, gdn_conv[l], gdn_A_log[l], gdn_dt_bias[l])
        o0, o1 = _gdn_scan(q, k, v, gcol, grow, ctx_len)
        ycv = _conformer_conv(P, conf_dw[l], ctx_len)
        last = l == depth - 1
        xs = _merge(xs, mods, ctx_row, y0, y1, bonus, gate, o0, o1, P, ycv,
                    rwkv_ln_g[l], rwkv_ln_b[l], gdn_norm[l], conf_dw_b[l], conf_ln_g[l], conf_ln_b[l],
                    w_branch[l].astype(BF16), w_out[l].astype(BF16), ctx_len, latent_only=last)
        ctx_rows = 0 if last else ctx_len
        n_rows = xs.shape[0] * xs.shape[1]
        h2, e, w = _ffn_pre(xs, norm_ffn[l], mods, ctx_row, w_router, router_bias, ctx_rows)
        e = e.transpose(1, 0, 2).reshape(2, n_rows)
        w = w.transpose(1, 0, 2).reshape(2, n_rows)
        row_src, row_dst, row_w, block_e = _moe_dispatch(e, w, n_rows)
        y = _moe(h2, row_src, row_dst, row_w, block_e,
                 w_e_gate[l].astype(BF16), w_e_up[l].astype(BF16), w_e_down[l].astype(BF16))
        if last:
            out = _final(xs, mods, y, norm_final, ctx_rows)
        else:
            xs = _ffn_add(xs, mods, ctx_row, y, ctx_len)
    return out
```
